```python
import math
import jax, jax.numpy as jnp
from jax import lax
import numpy as np

D_MODEL = 1024
BATCH = 4
SEQ = 4096
DEPTH = 2

CHUNK = 64
Q_BLOCK = 128
NORM_EPS = 1e-6

DIFF_HEADS = 4
DIFF_HEAD_DIM = 64
DIFF_V_DIM = 2 * DIFF_HEAD_DIM
DIFF_WIDTH = DIFF_HEADS * DIFF_V_DIM
DIFF_QK_COLS = DIFF_HEADS * 2 * DIFF_HEAD_DIM
CHK_HEADS = 8
CHK_HEAD_DIM = 64
CHK_WIDTH = CHK_HEADS * CHK_HEAD_DIM
CHK_LEFT_CHUNKS = 8
CHK_BAND = (CHK_LEFT_CHUNKS + 1) * CHUNK
REL_CLIP = 256
MLA_HEADS = 8
MLA_Q_LORA = 384
MLA_KV_LORA = 256
MLA_NOPE = 64
MLA_ROPE = 32
MLA_V = 64
MLA_WIDTH = MLA_HEADS * MLA_V
ROPE_THETA = 10000.0
N_BRANCHES = 3
IN_SPLITS = (DIFF_QK_COLS, DIFF_QK_COLS, DIFF_WIDTH,
             CHK_WIDTH, CHK_WIDTH, CHK_WIDTH,
             MLA_Q_LORA, MLA_KV_LORA + MLA_ROPE,
             N_BRANCHES * D_MODEL)
IN_WIDTH = sum(IN_SPLITS)
N_EXPERTS = 16
N_GROUPS = 4
EXPERTS_PER_GROUP = N_EXPERTS // N_GROUPS
TOP_K = 2
MOE_D_FF = 512

kernel_name = "hybrid_chunk_causal_diff_chunk_mla_grouped_moe"


def rms_norm(x, g, eps=NORM_EPS):
    xf = x.astype(jnp.float32)
    y = xf * lax.rsqrt(jnp.mean(xf * xf, axis=-1, keepdims=True) + eps)
    return (y * g.astype(jnp.float32)).astype(x.dtype)


def apply_rope(x, cos, sin):
    xf = x.astype(jnp.float32)
    x1, x2 = jnp.split(xf, 2, axis=-1)
    return jnp.concatenate([x1 * cos - x2 * sin, x1 * sin + x2 * cos], axis=-1).astype(x.dtype)


def to_blocks(t):
    b, s = t.shape[:2]
    return jnp.moveaxis(t.reshape(b, s // Q_BLOCK, Q_BLOCK, *t.shape[2:]), 1, 0)


def from_blocks(t):
    nb, b, qb = t.shape[:3]
    return jnp.moveaxis(t, 0, 1).reshape(b, nb * qb, *t.shape[3:])


def chunk_causal_mask(blk, seq):
    q_pos = blk * Q_BLOCK + jnp.arange(Q_BLOCK)
    k_pos = jnp.arange(seq)
    allowed = (k_pos[None, :] // CHUNK) <= (q_pos[:, None] // CHUNK)
    return allowed, q_pos, k_pos


def diff_attention(q, k, v, lam, slopes):
    seq = q.shape[1]
    scale = DIFF_HEAD_DIM ** -0.5

    def one_block(args):
        qb, blk = args
        allowed, q_pos, k_pos = chunk_causal_mask(blk, seq)
        dist = jnp.abs(q_pos[:, None] - k_pos[None, :]).astype(jnp.float32)
        alibi = -slopes[:, None, None] * dist[None]
        s = jnp.einsum('bqhcd,bkhcd->bhcqk', qb, k).astype(jnp.float32) * scale
        s = jnp.where(allowed, s + alibi[None, :, None], -jnp.inf)
        p = jax.nn.softmax(s, axis=-1)
        a = p[:, :, 0] - lam * p[:, :, 1]
        return jnp.einsum('bhqk,bkhe->bqhe', a.astype(v.dtype), v)

    out = lax.map(one_block, (to_blocks(q), jnp.arange(seq // Q_BLOCK)))
    return from_blocks(out)


def chunked_band_attention(q, k, v, rel_table):
    b, seq, h, d = q.shape
    nc = seq // CHUNK
    qc = q.reshape(b, nc, CHUNK, h, d)
    pad = ((0, 0), (CHK_LEFT_CHUNKS, 0), (0, 0), (0, 0), (0, 0))
    kp = jnp.pad(k.reshape(b, nc, CHUNK, h, d), pad)
    vp = jnp.pad(v.reshape(b, nc, CHUNK, h, d), pad)
    band_idx = jnp.arange(nc)[:, None] + jnp.arange(CHK_LEFT_CHUNKS + 1)[None, :]
    kb = kp[:, band_idx].reshape(b, nc, CHK_BAND, h, d)
    vb = vp[:, band_idx].reshape(b, nc, CHK_BAND, h, d)
    q_i = jnp.arange(CHUNK)
    k_i = jnp.arange(CHK_BAND)
    rel = jnp.clip(q_i[:, None] + CHK_LEFT_CHUNKS * CHUNK - k_i[None, :], -REL_CLIP, REL_CLIP) + REL_CLIP
    bias = rel_table[:, rel].astype(jnp.float32)
    src_chunk = jnp.arange(nc)[:, None] - CHK_LEFT_CHUNKS + (k_i // CHUNK)[None, :]
    valid = src_chunk >= 0
    s = jnp.einsum('bnqhd,bnkhd->bhnqk', qc, kb).astype(jnp.float32) * (d ** -0.5)
    s = jnp.where(valid[None, None, :, None, :], s + bias[None, :, None], -jnp.inf)
    p = jax.nn.softmax(s, axis=-1)
    o = jnp.einsum('bhnqk,bnkhd->bnqhd', p.astype(v.dtype), vb)
    return o.reshape(b, seq, h, d)


def mla_attention(q_nope, q_rope, k_nope, k_rope, v):
    seq = q_nope.shape[1]
    scale = (MLA_NOPE + MLA_ROPE) ** -0.5

    def one_block(args):
        qn, qr, blk = args
        allowed, _, _ = chunk_causal_mask(blk, seq)
        s = (jnp.einsum('bqhd,bkhd->bhqk', qn, k_nope)
             + jnp.einsum('bqhr,bkr->bhqk', qr, k_rope)).astype(jnp.float32) * scale
        s = jnp.where(allowed, s, -jnp.inf)
        p = jax.nn.softmax(s, axis=-1)
        return jnp.einsum('bhqk,bkhd->bqhd', p.astype(v.dtype), v)

    out = lax.map(one_block, (to_blocks(q_nope), to_blocks(q_rope), jnp.arange(seq // Q_BLOCK)))
    return from_blocks(out)


def token_mixer(h, layer, cos, sin, w_in, lq1, lk1, lq2, lk2, subln_g, rel_table,
                q_norm_g, w_q_b, kv_norm_g, w_kv_b, w_br_diff, w_br_chunk, w_br_mla, w_out):
    b, seq, _ = h.shape
    f32 = jnp.float32
    split_points = np.cumsum(IN_SPLITS)[:-1].tolist()
    dq, dk, dv, cq, ck, cv, mq, mkv, gate_logits = jnp.split(h @ w_in, split_points, axis=-1)

    lam_init = 0.8 - 0.6 * math.exp(-0.3 * layer)
    lam = (jnp.exp(jnp.sum(lq1.astype(f32) * lk1.astype(f32)))
           - jnp.exp(jnp.sum(lq2.astype(f32) * lk2.astype(f32))) + lam_init)
    slopes = jnp.exp2(-8.0 / DIFF_HEADS * jnp.arange(1, DIFF_HEADS + 1, dtype=f32))
    y_a = diff_attention(dq.reshape(b, seq, DIFF_HEADS, 2, DIFF_HEAD_DIM),
                         dk.reshape(b, seq, DIFF_HEADS, 2, DIFF_HEAD_DIM),
                         dv.reshape(b, seq, DIFF_HEADS, DIFF_V_DIM), lam, slopes)
    y_a = (rms_norm(y_a, subln_g) * (1.0 - lam_init)).reshape(b, seq, DIFF_WIDTH)

    y_b = chunked_band_attention(cq.reshape(b, seq, CHK_HEADS, CHK_HEAD_DIM),
                                 ck.reshape(b, seq, CHK_HEADS, CHK_HEAD_DIM),
                                 cv.reshape(b, seq, CHK_HEADS, CHK_HEAD_DIM), rel_table)
    y_b = y_b.reshape(b, seq, CHK_WIDTH)

    q_full = (rms_norm(mq, q_norm_g) @ w_q_b).reshape(b, seq, MLA_HEADS, MLA_NOPE + MLA_ROPE)
    q_nope, q_rope = jnp.split(q_full, [MLA_NOPE], axis=-1)
    c_kv, k_rope = jnp.split(mkv, [MLA_KV_LORA], axis=-1)
    kv = (rms_norm(c_kv, kv_norm_g) @ w_kv_b).reshape(b, seq, MLA_HEADS, MLA_NOPE + MLA_V)
    k_nope, v_c = jnp.split(kv, [MLA_NOPE], axis=-1)
    q_rope = apply_rope(q_rope, cos[None, :, None, :], sin[None, :, None, :])
    k_rope = apply_rope(k_rope, cos[None], sin[None])
    y_c = mla_attention(q_nope, q_rope, k_nope, k_rope, v_c).reshape(b, seq, MLA_WIDTH)

    g = jax.nn.sigmoid(gate_logits.astype(f32)).astype(h.dtype).reshape(b, seq, N_BRANCHES, D_MODEL)
    merged = (g[:, :, 0] * (y_a @ w_br_diff) + g[:, :, 1] * (y_b @ w_br_chunk)
              + g[:, :, 2] * (y_c @ w_br_mla))
    return merged @ w_out


def grouped_moe(h, w_router, router_bias, w_gate, w_up, w_down):
    b, seq, d = h.shape
    t = h.reshape(-1, d)
    scores = jax.nn.sigmoid((t @ w_router).astype(jnp.float32))
    sel = scores + router_bias.astype(jnp.float32)
    grouped = sel.reshape(-1, N_GROUPS, EXPERTS_PER_GROUP)
    group_score = lax.top_k(grouped, TOP_K)[0].sum(-1)
    best_group = jnp.argmax(group_score, axis=-1)
    in_group = (jnp.arange(N_EXPERTS) // EXPERTS_PER_GROUP)[None, :] == best_group[:, None]
    _, idx = lax.top_k(jnp.where(in_group, sel, -jnp.inf), TOP_K)
    w = jnp.take_along_axis(scores, idx, axis=-1)
    w = w / jnp.sum(w, axis=-1, keepdims=True)
    combine = jnp.sum(jax.nn.one_hot(idx, N_EXPERTS, dtype=jnp.float32) * w[..., None], axis=1)
    combine = combine.astype(t.dtype)
    out = jnp.zeros(t.shape, t.dtype)
    for e in range(N_EXPERTS):
        he = jax.nn.silu(t @ w_gate[e]) * (t @ w_up[e])
        out = out + combine[:, e:e + 1] * (he @ w_down[e])
    return out.reshape(b, seq, d)


def setup_inputs(seed: int = 0) -> dict:
    key = jax.random.key(seed)
    keys = iter(jax.random.split(key, 40))
    f32 = jnp.float32

    def nrm(shape, scale):
        return scale * jax.random.normal(next(keys), shape, f32)

    def gain(shape):
        return 1.0 + nrm(shape, 0.05)

    L = DEPTH
    return {
        "x": nrm((BATCH, SEQ, D_MODEL), 1.0),
        "c": nrm((BATCH, D_MODEL), 1.0),
        "w_mod": nrm((L, D_MODEL, 6 * D_MODEL), 0.5 * D_MODEL ** -0.5),
        "b_mod": nrm((L, 6 * D_MODEL), 0.02),
        "g_norm_mix": gain((L, D_MODEL)),
        "g_norm_ffn": gain((L, D_MODEL)),
        "w_in": nrm((L, D_MODEL, IN_WIDTH), D_MODEL ** -0.5),
        "diff_lambda_q1": nrm((L, DIFF_HEAD_DIM), 0.1),
        "diff_lambda_k1": nrm((L, DIFF_HEAD_DIM), 0.1),
        "diff_lambda_q2": nrm((L, DIFF_HEAD_DIM), 0.1),
        "diff_lambda_k2": nrm((L, DIFF_HEAD_DIM), 0.1),
        "diff_subln_g": gain((L, DIFF_V_DIM)),
        "chunk_rel_bias": nrm((L, CHK_HEADS, 2 * REL_CLIP + 1), 0.2),
        "mla_q_norm_g": gain((L, MLA_Q_LORA)),
        "mla_w_q_b": nrm((L, MLA_Q_LORA, MLA_HEADS * (MLA_NOPE + MLA_ROPE)), MLA_Q_LORA ** -0.5),
        "mla_kv_norm_g": gain((L, MLA_KV_LORA)),
        "mla_w_kv_b": nrm((L, MLA_KV_LORA, MLA_HEADS * (MLA_NOPE + MLA_V)), MLA_KV_LORA ** -0.5),
        "w_branch_diff": nrm((L, DIFF_WIDTH, D_MODEL), DIFF_WIDTH ** -0.5),
        "w_branch_chunk": nrm((L, CHK_WIDTH, D_MODEL), CHK_WIDTH ** -0.5),
        "w_branch_mla": nrm((L, MLA_WIDTH, D_MODEL), MLA_WIDTH ** -0.5),
        "w_out": nrm((L, D_MODEL, D_MODEL), D_MODEL ** -0.5),
        "w_router": nrm((D_MODEL, N_EXPERTS), D_MODEL ** -0.5),
        "router_bias": nrm((N_EXPERTS,), 0.01),
        "w_exp_gate": nrm((L, N_EXPERTS, D_MODEL, MOE_D_FF), D_MODEL ** -0.5),
        "w_exp_up": nrm((L, N_EXPERTS, D_MODEL, MOE_D_FF), D_MODEL ** -0.5),
        "w_exp_down": nrm((L, N_EXPERTS, MOE_D_FF, D_MODEL), MOE_D_FF ** -0.5),
        "g_final": gain((D_MODEL,)),
    }


def reference(x, c, w_mod, b_mod, g_norm_mix, g_norm_ffn, w_in,
              diff_lambda_q1, diff_lambda_k1, diff_lambda_q2, diff_lambda_k2, diff_subln_g,
              chunk_rel_bias, mla_q_norm_g, mla_w_q_b, mla_kv_norm_g, mla_w_kv_b,
              w_branch_diff, w_branch_chunk, w_branch_mla, w_out,
              w_router, router_bias, w_exp_gate, w_exp_up, w_exp_down, g_final):
    seq = x.shape[1]
    pos = jnp.arange(seq, dtype=jnp.float32)
    inv_freq = 1.0 / (ROPE_THETA ** (jnp.arange(0, MLA_ROPE, 2, dtype=jnp.float32) / MLA_ROPE))
    ang = pos[:, None] * inv_freq[None, :]
    cos, sin = jnp.cos(ang), jnp.sin(ang)
    c_act = jax.nn.silu(c)
    for l in range(DEPTH):
        mod = c_act @ w_mod[l] + b_mod[l]
        sh1, sc1, gt1, sh2, sc2, gt2 = jnp.split(mod[:, None, :], 6, axis=-1)
        h = rms_norm(x, g_norm_mix[l]) * (1 + sc1) + sh1
        y = token_mixer(h, l, cos, sin, w_in[l],
                        diff_lambda_q1[l], diff_lambda_k1[l], diff_lambda_q2[l], diff_lambda_k2[l],
                        diff_subln_g[l], chunk_rel_bias[l],
                        mla_q_norm_g[l], mla_w_q_b[l], mla_kv_norm_g[l], mla_w_kv_b[l],
                        w_branch_diff[l], w_branch_chunk[l], w_branch_mla[l], w_out[l])
        x = x + gt1 * y
        h = rms_norm(x, g_norm_ffn[l]) * (1 + sc2) + sh2
        x = x + gt2 * grouped_moe(h, w_router, router_bias, w_exp_gate[l], w_exp_up[l], w_exp_down[l])
    return rms_norm(x, g_final)
```

```python
import math
from functools import partial

import jax
import jax.numpy as jnp
import numpy as np
from jax import lax
from jax.experimental import pallas as pl
from jax.experimental.pallas import tpu as pltpu

F32 = jnp.float32
BF16 = jnp.bfloat16

D_MODEL = 1024
CHUNK = 64
NORM_EPS = 1e-6
DIFF_HEADS = 4
DIFF_HEAD_DIM = 64
DIFF_V_DIM = 128
DIFF_WIDTH = 512
CHK_HEADS = 8
CHK_HEAD_DIM = 64
CHK_WIDTH = 512
CHK_LEFT_CHUNKS = 8
REL_CLIP = 256
MLA_HEADS = 8
MLA_Q_LORA = 384
MLA_KV_LORA = 256
MLA_NOPE = 64
MLA_ROPE = 32
MLA_V = 64
ROPE_THETA = 10000.0
N_EXPERTS = 16
N_GROUPS = 4
EXPERTS_PER_GROUP = 4
MOE_D_FF = 512

LANES = 128
NEG = -1e30
VMEM_LIMIT = 56 * 1024 * 1024

TM = 512
TQ_DIFF = 256
TQ_MLA = 256
TQ_CHK = 128
CHK_WIN = TQ_CHK + CHK_LEFT_CHUNKS * CHUNK
CHK_PAD = CHK_LEFT_CHUNKS * CHUNK
TM_MOE = 1024


def _cp(n_axes):
    return pltpu.CompilerParams(dimension_semantics=("arbitrary",) * n_axes,
                                vmem_limit_bytes=VMEM_LIMIT)


def _rms(x, g):
    return x * lax.rsqrt(jnp.mean(x * x, axis=-1, keepdims=True) + NORM_EPS) * g


def _mod_kernel(c_ref, w_ref, b_ref, o_ref):
    c = c_ref[...]
    c_act = c * jax.nn.sigmoid(c)
    o_ref[0] = jnp.dot(c_act.astype(BF16), w_ref[0].astype(BF16),
                       preferred_element_type=F32) + b_ref[0]


def _modulation(c, w_mod, b_mod):
    n_layers, d, n = w_mod.shape
    b = c.shape[0]
    tn = 1024
    return pl.pallas_call(
        _mod_kernel,
        grid=(n_layers, n // tn),
        in_specs=[pl.BlockSpec((b, d), lambda l, j: (0, 0)),
                  pl.BlockSpec((1, d, tn), lambda l, j: (l, 0, j)),
                  pl.BlockSpec((1, 1, tn), lambda l, j: (l, 0, j))],
        out_specs=pl.BlockSpec((1, b, tn), lambda l, j: (l, 0, j)),
        out_shape=jax.ShapeDtypeStruct((n_layers, b, n), F32),
        compiler_params=_cp(2),
        name="adaln_mod",
    )(c, w_mod, b_mod.reshape(n_layers, 1, n))


def _norm_mod_kernel(x_ref, g_ref, sc_ref, sh_ref, o_ref):
    h = _rms(x_ref[...], g_ref[...]) * (1.0 + sc_ref[0]) + sh_ref[0]
    o_ref[...] = h.astype(o_ref.dtype)


def _norm_mod(x2, g, sc, sh, seq):
    t, d = x2.shape
    tpb = seq // TM
    bspec = pl.BlockSpec((1, 1, d), lambda i: (i // tpb, 0, 0))
    return pl.pallas_call(
        _norm_mod_kernel,
        grid=(t // TM,),
        in_specs=[pl.BlockSpec((TM, d), lambda i: (i, 0)),
                  pl.BlockSpec((1, d), lambda i: (0, 0)), bspec, bspec],
        out_specs=pl.BlockSpec((TM, d), lambda i: (i, 0)),
        out_shape=jax.ShapeDtypeStruct((t, d), BF16),
        compiler_params=_cp(1),
        name="norm_mod",
    )(x2, g.reshape(1, d), sc, sh)


def _proj_kernel(h_ref, w_ref, o_ref, *, tn, sigmoid):
    h = h_ref[...]
    for j in range(w_ref.shape[1] // tn):
        r = jnp.dot(h, w_ref[:, j * tn:(j + 1) * tn], preferred_element_type=F32)
        if sigmoid:
            r = jax.nn.sigmoid(r)
        o_ref[:, j * tn:(j + 1) * tn] = r.astype(o_ref.dtype)


def _proj(h, w, sigmoid=False, name="proj"):
    t, d = h.shape
    n = w.shape[1]
    return pl.pallas_call(
        partial(_proj_kernel, tn=512, sigmoid=sigmoid),
        grid=(t // TM,),
        in_specs=[pl.BlockSpec((TM, d), lambda i: (i, 0)),
                  pl.BlockSpec((d, n), lambda i: (0, 0))],
        out_specs=pl.BlockSpec((TM, n), lambda i: (i, 0)),
        out_shape=jax.ShapeDtypeStruct((t, n), BF16),
        compiler_params=_cp(1),
        name=name,
    )(h, w)


def _mla_prep_kernel(h_ref, wmq_ref, wckv_ref, wkr_ref, wkrs_ref, gq_ref, gkv_ref,
                     wq_ref, wqs_ref, wk_ref, wv_ref, cos_ref, sin_ref,
                     q_ref, k_ref, v_ref, *, scale):
    h = h_ref[...]
    cos = cos_ref[...]
    sin = sin_ref[...]
    mq = jnp.dot(h, wmq_ref[...], preferred_element_type=F32)
    qn = _rms(mq, gq_ref[...]).astype(BF16)
    ckv = jnp.dot(h, wckv_ref[...], preferred_element_type=F32)
    cn = _rms(ckv, gkv_ref[...]).astype(BF16)
    kr = (jnp.dot(h, wkr_ref[...], preferred_element_type=F32) * cos
          + jnp.dot(h, wkrs_ref[...], preferred_element_type=F32) * sin)
    for hd in range(MLA_HEADS):
        cols = slice(hd * LANES, (hd + 1) * LANES)
        q = (jnp.dot(qn, wq_ref[:, cols], preferred_element_type=F32) * cos
             + jnp.dot(qn, wqs_ref[:, cols], preferred_element_type=F32) * sin)
        q_ref[:, cols] = (q * scale).astype(q_ref.dtype)
        k = jnp.dot(cn, wk_ref[:, cols], preferred_element_type=F32) + kr
        k_ref[:, cols] = k.astype(k_ref.dtype)
    v_ref[...] = jnp.dot(cn, wv_ref[...], preferred_element_type=F32).astype(v_ref.dtype)


def _mla_prep(h, w, cos_tab, sin_tab, seq):
    t, d = h.shape
    tpb = seq // TM
    full = lambda a: pl.BlockSpec(a.shape, lambda i: (0,) * a.ndim)
    tab = pl.BlockSpec((TM, LANES), lambda i: (i % tpb, 0))
    ws = [w["mq"], w["ckv"], w["kr"], w["krs"], w["gq"], w["gkv"], w["q"], w["qs"], w["k"], w["v"]]
    scale = (MLA_NOPE + MLA_ROPE) ** -0.5
    return pl.pallas_call(
        partial(_mla_prep_kernel, scale=scale),
        grid=(t // TM,),
        in_specs=[pl.BlockSpec((TM, d), lambda i: (i, 0))] + [full(a) for a in ws] + [tab, tab],
        out_specs=[pl.BlockSpec((TM, MLA_HEADS * LANES), lambda i: (i, 0)),
                   pl.BlockSpec((TM, MLA_HEADS * LANES), lambda i: (i, 0)),
                   pl.BlockSpec((TM, MLA_HEADS * MLA_V), lambda i: (i, 0))],
        out_shape=[jax.ShapeDtypeStruct((t, MLA_HEADS * LANES), BF16),
                   jax.ShapeDtypeStruct((t, MLA_HEADS * LANES), BF16),
                   jax.ShapeDtypeStruct((t, MLA_HEADS * MLA_V), BF16)],
        compiler_params=_cp(1),
        name="mla_prep",
    )(h, *ws, cos_tab, sin_tab)


def _flash_step(s, v, carry):
    m, l, acc = carry
    m_new = jnp.maximum(m, jnp.max(s, axis=-1, keepdims=True))
    alpha = jnp.exp(m - m_new)
    p = jnp.exp(s - m_new)
    l = alpha * l + jnp.sum(p, axis=-1, keepdims=True)
    acc = alpha * acc + jnp.dot(p.astype(v.dtype), v, preferred_element_type=F32)
    return m_new, l, acc


def _nt_dot(a, b):
    return lax.dot_general(a, b, (((1,), (1,)), ((), ())), preferred_element_type=F32)


def _half_masks(rows):
    lane = lax.broadcasted_iota(jnp.int32, (rows, LANES), 1)
    lo = (lane < LANES // 2)
    return lo, jnp.logical_not(lo)


def _masked_halves(q, scale, lo, hi):
    qf = q.astype(F32) * scale
    return jnp.where(lo, qf, 0.0).astype(BF16), jnp.where(hi, qf, 0.0).astype(BF16)


def _diff_attn_kernel(q_ref, k_ref, v_ref, lq1_ref, lk1_ref, lq2_ref, lk2_ref, g_ref, o_ref,
                      *, tq, lam_init):
    qi = pl.program_id(1)
    lam = (jnp.exp(jnp.sum(lq1_ref[...] * lk1_ref[...], axis=-1, keepdims=True))
           - jnp.exp(jnp.sum(lq2_ref[...] * lk2_ref[...], axis=-1, keepdims=True)) + lam_init)
    scale = DIFF_HEAD_DIM ** -0.5
    lo, hi = _half_masks(tq)
    ri = lax.broadcasted_iota(jnp.int32, (tq, tq), 0)
    ci = lax.broadcasted_iota(jnp.int32, (tq, tq), 1)
    rel = (ri - ci).astype(F32)
    diag_ok = (ci // CHUNK) <= (ri // CHUNK)

    for hd in range(DIFF_HEADS):
        slope = 2.0 ** (-8.0 / DIFF_HEADS * (hd + 1))
        cols = slice(hd * LANES, (hd + 1) * LANES)
        q0, q1 = _masked_halves(q_ref[:, cols], scale, lo, hi)
        off_bias = -slope * rel
        diag_bias = jnp.where(diag_ok, -slope * jnp.abs(rel), NEG)

        def step(kt, carry, bias):
            c0, c1 = carry
            ks = pl.multiple_of(kt * tq, tq)
            kh = k_ref[pl.ds(ks, tq), cols]
            vh = v_ref[pl.ds(ks, tq), cols]
            c0 = _flash_step(_nt_dot(q0, kh) + bias, vh, c0)
            c1 = _flash_step(_nt_dot(q1, kh) + bias, vh, c1)
            return c0, c1

        def off_step(kt, carry):
            shift = (-slope * tq) * (qi - kt).astype(F32)
            return step(kt, carry, off_bias + shift)

        init = (jnp.full((tq, 1), NEG, F32), jnp.zeros((tq, 1), F32), jnp.zeros((tq, DIFF_V_DIM), F32))
        carry = lax.fori_loop(0, qi, off_step, (init, init))
        (_, l0, a0), (_, l1, a1) = step(qi, carry, diag_bias)
        o = a0 / l0 - lam * (a1 / l1)
        y = _rms(o, g_ref[...]) * (1.0 - lam_init)
        o_ref[:, cols] = y.astype(o_ref.dtype)


def _diff_attn(qkv, lq1, lk1, lq2, lk2, g, batch, seq, lam_init):
    t = qkv.shape[0]
    tq = TQ_DIFF
    nq = seq // tq
    vec = lambda a: pl.BlockSpec((1, a.shape[-1]), lambda b, i: (0, 0))
    args = [a.reshape(1, -1) for a in (lq1, lk1, lq2, lk2, g)]
    return pl.pallas_call(
        partial(_diff_attn_kernel, tq=tq, lam_init=lam_init),
        grid=(batch, nq),
        in_specs=[pl.BlockSpec((tq, DIFF_WIDTH), lambda b, i: (b * nq + i, 0)),
                  pl.BlockSpec((seq, DIFF_WIDTH), lambda b, i: (b, 1)),
                  pl.BlockSpec((seq, DIFF_WIDTH), lambda b, i: (b, 2))] + [vec(a) for a in args],
        out_specs=pl.BlockSpec((tq, DIFF_WIDTH), lambda b, i: (b * nq + i, 0)),
        out_shape=jax.ShapeDtypeStruct((t, DIFF_WIDTH), BF16),
        compiler_params=_cp(2),
        name="diff_attn",
    )(qkv, qkv, qkv, *args)


def _chunk_attn_kernel(q_ref, k_ref, v_ref, bias_ref, o_ref, ks_ref, vs_ref, *, seq):
    qi = pl.program_id(1)

    @pl.when(qi == 0)
    def _():
        ks_ref[0:CHK_PAD, :] = jnp.zeros((CHK_PAD, CHK_WIDTH), ks_ref.dtype)
        vs_ref[0:CHK_PAD, :] = jnp.zeros((CHK_PAD, CHK_WIDTH), vs_ref.dtype)
        ks_ref[CHK_PAD:CHK_PAD + seq, :] = k_ref[...]
        vs_ref[CHK_PAD:CHK_PAD + seq, :] = v_ref[...]

    q0 = pl.multiple_of(qi * TQ_CHK, TQ_CHK)
    scale = CHK_HEAD_DIM ** -0.5
    lo, hi = _half_masks(TQ_CHK)
    col = lax.broadcasted_iota(jnp.int32, (TQ_CHK, CHK_WIN), 1)
    in_seq = col >= (CHK_PAD - q0)
    for pair in range(CHK_HEADS // 2):
        cols = slice(pair * LANES, (pair + 1) * LANES)
        q_halves = _masked_halves(q_ref[:, cols], scale, lo, hi)
        kp = ks_ref[pl.ds(q0, CHK_WIN), cols]
        vp = vs_ref[pl.ds(q0, CHK_WIN), cols]
        outs = []
        for half in range(2):
            s = _nt_dot(q_halves[half], kp) + bias_ref[2 * pair + half]
            s = jnp.where(in_seq, s, NEG)
            m = jnp.max(s, axis=-1, keepdims=True)
            p = jnp.exp(s - m)
            l = jnp.sum(p, axis=-1, keepdims=True)
            outs.append(jnp.dot(p.astype(vp.dtype), vp, preferred_element_type=F32) / l)
        o_ref[:, cols] = jnp.where(lo, outs[0], outs[1]).astype(o_ref.dtype)


def _chunk_attn(qkv, bias, batch, seq):
    t = qkv.shape[0]
    nq = seq // TQ_CHK
    return pl.pallas_call(
        partial(_chunk_attn_kernel, seq=seq),
        grid=(batch, nq),
        in_specs=[pl.BlockSpec((TQ_CHK, CHK_WIDTH), lambda b, i: (b * nq + i, 0)),
                  pl.BlockSpec((seq, CHK_WIDTH), lambda b, i: (b, 1)),
                  pl.BlockSpec((seq, CHK_WIDTH), lambda b, i: (b, 2)),
                  pl.BlockSpec(bias.shape, lambda b, i: (0, 0, 0))],
        out_specs=pl.BlockSpec((TQ_CHK, CHK_WIDTH), lambda b, i: (b * nq + i, 0)),
        out_shape=jax.ShapeDtypeStruct((t, CHK_WIDTH), BF16),
        scratch_shapes=[pltpu.VMEM((CHK_PAD + seq, CHK_WIDTH), BF16),
                        pltpu.VMEM((CHK_PAD + seq, CHK_WIDTH), BF16)],
        compiler_params=_cp(2),
        name="chunk_attn",
    )(qkv, qkv, qkv, bias)


def _chunk_bias(rel_table):
    qi = np.arange(TQ_CHK)[:, None]
    kj = np.arange(CHK_WIN)[None, :]
    rel = np.clip(qi + CHK_PAD - kj, -REL_CLIP, REL_CLIP) + REL_CLIP
    dchunk = (qi + CHK_PAD) // CHUNK - kj // CHUNK
    band = (dchunk >= 0) & (dchunk <= CHK_LEFT_CHUNKS)
    return jnp.where(jnp.asarray(band)[None], rel_table[:, rel].astype(F32), NEG)


def _mla_attn_kernel(q_ref, k_ref, v_ref, o_ref, *, tq):
    qi = pl.program_id(1)
    lo, _ = _half_masks(tq)
    ri = lax.broadcasted_iota(jnp.int32, (tq, tq), 0)
    ci = lax.broadcasted_iota(jnp.int32, (tq, tq), 1)
    diag_ok = (ci // CHUNK) <= (ri // CHUNK)
    init = (jnp.full((tq, 1), NEG, F32), jnp.zeros((tq, 1), F32), jnp.zeros((tq, LANES), F32))
    for pair in range(MLA_HEADS // 2):
        vcols = slice(pair * LANES, (pair + 1) * LANES)
        outs = []
        for half in range(2):
            hd = 2 * pair + half
            cols = slice(hd * LANES, (hd + 1) * LANES)
            q = q_ref[:, cols]

            def step(kt, carry, masked):
                ks = pl.multiple_of(kt * tq, tq)
                s = _nt_dot(q, k_ref[pl.ds(ks, tq), cols])
                if masked:
                    s = jnp.where(diag_ok, s, NEG)
                return _flash_step(s, v_ref[pl.ds(ks, tq), vcols], carry)

            carry = lax.fori_loop(0, qi, partial(step, masked=False), init)
            _, l, acc = step(qi, carry, True)
            outs.append(acc / l)
        o_ref[:, vcols] = jnp.where(lo, outs[0], outs[1]).astype(o_ref.dtype)


def _mla_attn(q, k, v, batch, seq):
    t = q.shape[0]
    tq = TQ_MLA
    nq = seq // tq
    return pl.pallas_call(
        partial(_mla_attn_kernel, tq=tq),
        grid=(batch, nq),
        in_specs=[pl.BlockSpec((tq, q.shape[1]), lambda b, i: (b * nq + i, 0)),
                  pl.BlockSpec((seq, k.shape[1]), lambda b, i: (b, 0)),
                  pl.BlockSpec((seq, v.shape[1]), lambda b, i: (b, 0))],
        out_specs=pl.BlockSpec((tq, v.shape[1]), lambda b, i: (b * nq + i, 0)),
        out_shape=jax.ShapeDtypeStruct((t, v.shape[1]), BF16),
        compiler_params=_cp(2),
        name="mla_attn",
    )(q, k, v)


def _route(logits_t, bias_ref):
    e_rows = [logits_t[e:e + 1, :] for e in range(N_EXPERTS)]
    scores = [jax.nn.sigmoid(r) for r in e_rows]
    sel = [scores[e] + bias_ref[e] for e in range(N_EXPERTS)]
    gscore = []
    for g in range(N_GROUPS):
        s4 = sel[g * EXPERTS_PER_GROUP:(g + 1) * EXPERTS_PER_GROUP]
        best = None
        for a in range(EXPERTS_PER_GROUP):
            for b in range(a + 1, EXPERTS_PER_GROUP):
                pair = s4[a] + s4[b]
                best = pair if best is None else jnp.maximum(best, pair)
        gscore.append(best)
    gbest = gscore[0]
    gidx = jnp.zeros_like(gbest, dtype=jnp.int32)
    for g in range(1, N_GROUPS):
        better = gscore[g] > gbest
        gbest = jnp.where(better, gscore[g], gbest)
        gidx = jnp.where(better, g, gidx)
    masked = [jnp.where(gidx == e // EXPERTS_PER_GROUP, sel[e], -jnp.inf) for e in range(N_EXPERTS)]

    def arg_first_max(vals):
        top = vals[0]
        for v in vals[1:]:
            top = jnp.maximum(top, v)
        idx = jnp.full(top.shape, N_EXPERTS, jnp.int32)
        for e in range(N_EXPERTS - 1, -1, -1):
            idx = jnp.where(vals[e] == top, e, idx)
        return idx

    i0 = arg_first_max(masked)
    i1 = arg_first_max([jnp.where(i0 == e, -jnp.inf, masked[e]) for e in range(N_EXPERTS)])
    w0 = sum(jnp.where(i0 == e, scores[e], 0.0) for e in range(N_EXPERTS))
    w1 = sum(jnp.where(i1 == e, scores[e], 0.0) for e in range(N_EXPERTS))
    den = w0 + w1
    w0 = w0 / den
    w1 = w1 / den
    rows = [jnp.where(i0 == e, w0, 0.0) + jnp.where(i1 == e, w1, 0.0) for e in range(N_EXPERTS)]
    return jnp.concatenate(rows, axis=0)


def _merge_kernel(ya_ref, yb_ref, yc_ref, gate_ref, x_ref, gt_ref, sc_ref, sh_ref, g_ref,
                  wa_ref, wb_ref, wc_ref, wo_ref, wr_ref, rb_ref,
                  xo_ref, h_ref, comb_ref):
    d = D_MODEL
    merged = (gate_ref[:, 0:d].astype(F32) * jnp.dot(ya_ref[...], wa_ref[...], preferred_element_type=F32)
              + gate_ref[:, d:2 * d].astype(F32) * jnp.dot(yb_ref[...], wb_ref[...], preferred_element_type=F32)
              + gate_ref[:, 2 * d:3 * d].astype(F32) * jnp.dot(yc_ref[...], wc_ref[...], preferred_element_type=F32))
    y = jnp.dot(merged.astype(BF16), wo_ref[...], preferred_element_type=F32)
    x_new = x_ref[...] + gt_ref[0] * y
    xo_ref[...] = x_new
    h2 = _rms(x_new, g_ref[...]) * (1.0 + sc_ref[0]) + sh_ref[0]
    h_ref[...] = h2.astype(h_ref.dtype)
    logits = jnp.dot(h2, wr_ref[...], preferred_element_type=F32, precision=lax.Precision.HIGHEST)
    comb_ref[...] = _route(logits.T[0:N_EXPERTS, :], rb_ref)


def _merge(ya, yb, yc, gates, x2, gt, sc, sh, g, w, wr_pad, rbias, seq):
    t, d = x2.shape
    tm = 256
    tpb = seq // tm
    row = lambda n: pl.BlockSpec((tm, n), lambda i: (i, 0))
    full = lambda a: pl.BlockSpec(a.shape, lambda i: (0,) * a.ndim)
    bspec = pl.BlockSpec((1, 1, d), lambda i: (i // tpb, 0, 0))
    ws = [w["br_diff"], w["br_chunk"], w["br_mla"], w["out"], wr_pad]
    g2 = g.reshape(1, d)
    return pl.pallas_call(
        _merge_kernel,
        grid=(t // tm,),
        in_specs=[row(ya.shape[1]), row(yb.shape[1]), row(yc.shape[1]), row(gates.shape[1]), row(d),
                  bspec, bspec, bspec, full(g2)] + [full(a) for a in ws]
                 + [pl.BlockSpec(memory_space=pltpu.SMEM)],
        out_specs=[row(d), row(d), pl.BlockSpec((N_EXPERTS, tm), lambda i: (0, i))],
        out_shape=[jax.ShapeDtypeStruct((t, d), F32), jax.ShapeDtypeStruct((t, d), BF16),
                   jax.ShapeDtypeStruct((N_EXPERTS, t), F32)],
        compiler_params=_cp(1),
        name="merge_route",
    )(ya, yb, yc, gates, x2, gt, sc, sh, g2, *ws, rbias)


def _moe_kernel(h_ref, wg_ref, wu_ref, wd_ref, comb_ref, x_ref, gt_ref, o_ref, acc_ref):
    e = pl.program_id(1)

    @pl.when(e == 0)
    def _():
        acc_ref[...] = jnp.zeros_like(acc_ref)

    h = h_ref[...]
    a = jnp.dot(h, wg_ref[0], preferred_element_type=F32)
    b = jnp.dot(h, wu_ref[0], preferred_element_type=F32)
    he = (a * jax.nn.sigmoid(a) * b).astype(BF16)
    acc_ref[...] += comb_ref[0] * jnp.dot(he, wd_ref[0], preferred_element_type=F32)

    @pl.when(e == pl.num_programs(1) - 1)
    def _():
        o_ref[...] = x_ref[...] + gt_ref[0] * acc_ref[...]


def _moe_dense(h2, comb, x2, gt, wg, wu, wd, seq):
    t, d = x2.shape
    tm = TM_MOE
    tpb = seq // tm
    n_e, _, dff = wg.shape
    return pl.pallas_call(
        _moe_kernel,
        grid=(t // tm, n_e),
        in_specs=[pl.BlockSpec((tm, d), lambda i, e: (i, 0)),
                  pl.BlockSpec((1, d, dff), lambda i, e: (e, 0, 0)),
                  pl.BlockSpec((1, d, dff), lambda i, e: (e, 0, 0)),
                  pl.BlockSpec((1, dff, d), lambda i, e: (e, 0, 0)),
                  pl.BlockSpec((1, tm, 1), lambda i, e: (e, i, 0)),
                  pl.BlockSpec((tm, d), lambda i, e: (i, 0)),
                  pl.BlockSpec((1, 1, d), lambda i, e: (i // tpb, 0, 0))],
        out_specs=pl.BlockSpec((tm, d), lambda i, e: (i, 0)),
        out_shape=jax.ShapeDtypeStruct((t, d), F32),
        scratch_shapes=[pltpu.VMEM((tm, d), F32)],
        compiler_params=_cp(2),
        name="moe_dense",
    )(h2, wg, wu, wd, comb, x2, gt)


def _final_norm_kernel(x_ref, g_ref, o_ref):
    o_ref[...] = _rms(x_ref[...], g_ref[...])


def _final_norm(x2, g):
    t, d = x2.shape
    return pl.pallas_call(
        _final_norm_kernel,
        grid=(t // TM,),
        in_specs=[pl.BlockSpec((TM, d), lambda i: (i, 0)), pl.BlockSpec((1, d), lambda i: (0, 0))],
        out_specs=pl.BlockSpec((TM, d), lambda i: (i, 0)),
        out_shape=jax.ShapeDtypeStruct((t, d), F32),
        compiler_params=_cp(1),
        name="final_norm",
    )(x2, g.reshape(1, d))


def _pad_heads(w, n_heads, width, start, stop, at=0):
    rows = w.shape[0]
    wh = w.reshape(rows, n_heads, width)[:, :, start:stop]
    out = jnp.zeros((rows, n_heads, LANES), w.dtype)
    out = out.at[:, :, at:at + (stop - start)].set(wh)
    return out.reshape(rows, n_heads * LANES)


def _rope_swap(w_rope):
    half = MLA_ROPE // 2
    return jnp.concatenate([-w_rope[..., half:], w_rope[..., :half]], axis=-1)


def _layer_weights(l, w_in, mla_q_norm_g, mla_w_q_b, mla_kv_norm_g, mla_w_kv_b,
                   w_branch_diff, w_branch_chunk, w_branch_mla, w_out):
    wi = w_in[l]
    o_chunk = 3 * DIFF_WIDTH
    o_mq = o_chunk + 3 * CHK_WIDTH
    o_mkv = o_mq + MLA_Q_LORA
    o_gate = o_mkv + MLA_KV_LORA + MLA_ROPE
    qk = MLA_NOPE + MLA_ROPE
    wqb = mla_w_q_b[l]
    wq_rope = wqb.reshape(MLA_Q_LORA, MLA_HEADS, qk)[:, :, MLA_NOPE:]
    wqs = jnp.zeros((MLA_Q_LORA, MLA_HEADS, LANES), F32).at[:, :, MLA_NOPE:qk].set(_rope_swap(wq_rope))
    wkr = wi[:, o_mkv + MLA_KV_LORA:o_gate]
    pad_kr = lambda a: jnp.zeros((D_MODEL, LANES), F32).at[:, MLA_NOPE:qk].set(a)
    wkvb = mla_w_kv_b[l]
    bf = lambda a: a.astype(BF16)
    return {
        "diff": bf(wi[:, :o_chunk]),
        "chunk": bf(wi[:, o_chunk:o_mq]),
        "gate": bf(wi[:, o_gate:]),
        "mq": bf(wi[:, o_mq:o_mkv]),
        "ckv": bf(wi[:, o_mkv:o_mkv + MLA_KV_LORA]),
        "kr": bf(pad_kr(wkr)),
        "krs": bf(pad_kr(_rope_swap(wkr))),
        "gq": mla_q_norm_g[l].reshape(1, -1),
        "gkv": mla_kv_norm_g[l].reshape(1, -1),
        "q": bf(_pad_heads(wqb, MLA_HEADS, qk, 0, qk)),
        "qs": bf(wqs.reshape(MLA_Q_LORA, MLA_HEADS * LANES)),
        "k": bf(_pad_heads(wkvb, MLA_HEADS, MLA_NOPE + MLA_V, 0, MLA_NOPE)),
        "v": bf(wkvb.reshape(MLA_KV_LORA, MLA_HEADS, MLA_NOPE + MLA_V)[:, :, MLA_NOPE:]
                .reshape(MLA_KV_LORA, MLA_HEADS * MLA_V)),
        "br_diff": bf(w_branch_diff[l]),
        "br_chunk": bf(w_branch_chunk[l]),
        "br_mla": bf(w_branch_mla[l]),
        "out": bf(w_out[l]),
    }


def _rope_tables(seq):
    pos = jnp.arange(seq, dtype=F32)
    inv_freq = 1.0 / (ROPE_THETA ** (jnp.arange(0, MLA_ROPE, 2, dtype=F32) / MLA_ROPE))
    ang = pos[:, None] * inv_freq[None, :]
    cos, sin = jnp.cos(ang), jnp.sin(ang)
    qk = MLA_NOPE + MLA_ROPE
    cos_tab = jnp.zeros((seq, LANES), F32).at[:, :MLA_NOPE].set(1.0)
    cos_tab = cos_tab.at[:, MLA_NOPE:qk].set(jnp.concatenate([cos, cos], axis=-1))
    sin_tab = jnp.zeros((seq, LANES), F32).at[:, MLA_NOPE:qk].set(jnp.concatenate([sin, sin], axis=-1))
    return cos_tab, sin_tab


def kernel(x, c, w_mod, b_mod, g_norm_mix, g_norm_ffn, w_in, diff_lambda_q1, diff_lambda_k1, diff_lambda_q2, diff_lambda_k2, diff_subln_g, chunk_rel_bias, mla_q_norm_g, mla_w_q_b, mla_kv_norm_g, mla_w_kv_b, w_branch_diff, w_branch_chunk, w_branch_mla, w_out, w_router, router_bias, w_exp_gate, w_exp_up, w_exp_down, g_final):
    batch, seq, d = x.shape
    depth = w_in.shape[0]
    t = batch * seq
    x2 = x.reshape(t, d)
    cos_tab, sin_tab = _rope_tables(seq)
    mod = _modulation(c, w_mod, b_mod)
    wr_pad = jnp.zeros((d, LANES), F32).at[:, :N_EXPERTS].set(w_router)
    for l in range(depth):
        sh1, sc1, gt1, sh2, sc2, gt2 = [m.reshape(batch, 1, d) for m in jnp.split(mod[l], 6, axis=-1)]
        w = _layer_weights(l, w_in, mla_q_norm_g, mla_w_q_b, mla_kv_norm_g, mla_w_kv_b,
                           w_branch_diff, w_branch_chunk, w_branch_mla, w_out)
        h = _norm_mod(x2, g_norm_mix[l], sc1, sh1, seq)
        lam_init = 0.8 - 0.6 * math.exp(-0.3 * l)
        ya = _diff_attn(_proj(h, w["diff"], name="proj_diff"),
                        diff_lambda_q1[l], diff_lambda_k1[l], diff_lambda_q2[l], diff_lambda_k2[l],
                        diff_subln_g[l], batch, seq, lam_init)
        yb = _chunk_attn(_proj(h, w["chunk"], name="proj_chunk"), _chunk_bias(chunk_rel_bias[l]), batch, seq)
        q, k, v = _mla_prep(h, w, cos_tab, sin_tab, seq)
        yc = _mla_attn(q, k, v, batch, seq)
        gates = _proj(h, w["gate"], sigmoid=True, name="proj_gate")
        x2, h2, comb_t = _merge(ya, yb, yc, gates, x2, gt1, sc2, sh2, g_norm_ffn[l], w, wr_pad,
                                router_bias, seq)
        comb = comb_t.reshape(N_EXPERTS, t, 1)
        x2 = _moe_dense(h2, comb, x2, gt2, w_exp_gate[l].astype(BF16), w_exp_up[l].astype(BF16),
                        w_exp_down[l].astype(BF16), seq)
    return _final_norm(x2, g_final).reshape(batch, seq, d)
```

```python
import math
from functools import partial

import jax
import jax.numpy as jnp
import numpy as np
from jax import lax
from jax.experimental import pallas as pl
from jax.experimental.pallas import tpu as pltpu

F32 = jnp.float32
BF16 = jnp.bfloat16

D_MODEL = 1024
CHUNK = 64
NORM_EPS = 1e-6
DIFF_HEADS = 4
DIFF_HEAD_DIM = 64
DIFF_V_DIM = 128
DIFF_WIDTH = 512
CHK_HEADS = 8
CHK_HEAD_DIM = 64
CHK_WIDTH = 512
CHK_LEFT_CHUNKS = 8
REL_CLIP = 256
MLA_HEADS = 8
MLA_Q_LORA = 384
MLA_KV_LORA = 256
MLA_NOPE = 64
MLA_ROPE = 32
MLA_V = 64
ROPE_THETA = 10000.0
N_EXPERTS = 16
N_GROUPS = 4
EXPERTS_PER_GROUP = 4
MOE_D_FF = 512

LANES = 128
NEG = -1e30
VMEM_LIMIT = 56 * 1024 * 1024

TM = 512
TKV = 256
ONES_ROWS = 16
MLA_VT_ROWS = MLA_V + ONES_ROWS
DIFF_VT_ROWS = DIFF_V_DIM + ONES_ROWS
LOG2E = math.log2(math.e)
TQ_CHK = 128
CHK_WIN = TQ_CHK + CHK_LEFT_CHUNKS * CHUNK
CHK_PAD = CHK_LEFT_CHUNKS * CHUNK
TM_MOE = 1024


def _cp(n_axes):
    return pltpu.CompilerParams(dimension_semantics=("arbitrary",) * n_axes,
                                vmem_limit_bytes=VMEM_LIMIT)


def _rms(x, g):
    return x * lax.rsqrt(jnp.mean(x * x, axis=-1, keepdims=True) + NORM_EPS) * g


def _mod_kernel(c_ref, w_ref, b_ref, o_ref):
    c = c_ref[...]
    c_act = c * jax.nn.sigmoid(c)
    o_ref[0] = jnp.dot(c_act.astype(BF16), w_ref[0].astype(BF16),
                       preferred_element_type=F32) + b_ref[0]


def _modulation(c, w_mod, b_mod):
    n_layers, d, n = w_mod.shape
    b = c.shape[0]
    tn = 1024
    return pl.pallas_call(
        _mod_kernel,
        grid=(n_layers, n // tn),
        in_specs=[pl.BlockSpec((b, d), lambda l, j: (0, 0)),
                  pl.BlockSpec((1, d, tn), lambda l, j: (l, 0, j)),
                  pl.BlockSpec((1, 1, tn), lambda l, j: (l, 0, j))],
        out_specs=pl.BlockSpec((1, b, tn), lambda l, j: (l, 0, j)),
        out_shape=jax.ShapeDtypeStruct((n_layers, b, n), F32),
        compiler_params=_cp(2),
        name="adaln_mod",
    )(c, w_mod, b_mod.reshape(n_layers, 1, n))


def _norm_mod_kernel(x_ref, g_ref, sc_ref, sh_ref, o_ref):
    h = _rms(x_ref[...], g_ref[...]) * (1.0 + sc_ref[0]) + sh_ref[0]
    o_ref[...] = h.astype(o_ref.dtype)


def _norm_mod(x2, g, sc, sh, seq):
    t, d = x2.shape
    tpb = seq // TM
    bspec = pl.BlockSpec((1, 1, d), lambda i: (i // tpb, 0, 0))
    return pl.pallas_call(
        _norm_mod_kernel,
        grid=(t // TM,),
        in_specs=[pl.BlockSpec((TM, d), lambda i: (i, 0)),
                  pl.BlockSpec((1, d), lambda i: (0, 0)), bspec, bspec],
        out_specs=pl.BlockSpec((TM, d), lambda i: (i, 0)),
        out_shape=jax.ShapeDtypeStruct((t, d), BF16),
        compiler_params=_cp(1),
        name="norm_mod",
    )(x2, g.reshape(1, d), sc, sh)


def _proj_kernel(h_ref, w_ref, o_ref, *, tn, sigmoid):
    h = h_ref[...]
    for j in range(w_ref.shape[1] // tn):
        r = jnp.dot(h, w_ref[:, j * tn:(j + 1) * tn], preferred_element_type=F32)
        if sigmoid:
            r = jax.nn.sigmoid(r)
        o_ref[:, j * tn:(j + 1) * tn] = r.astype(o_ref.dtype)


def _proj(h, w, sigmoid=False, name="proj"):
    t, d = h.shape
    n = w.shape[1]
    return pl.pallas_call(
        partial(_proj_kernel, tn=512, sigmoid=sigmoid),
        grid=(t // TM,),
        in_specs=[pl.BlockSpec((TM, d), lambda i: (i, 0)),
                  pl.BlockSpec((d, n), lambda i: (0, 0))],
        out_specs=pl.BlockSpec((TM, n), lambda i: (i, 0)),
        out_shape=jax.ShapeDtypeStruct((t, n), BF16),
        compiler_params=_cp(1),
        name=name,
    )(h, w)


def _mla_prep_kernel(h_ref, wmq_ref, wckv_ref, wkr_ref, wkrs_ref, gq_ref, gkv_ref,
                     wq_ref, wqs_ref, wk_ref, wvt_ref, cos_ref, sin_ref,
                     q_ref, k_ref, vt_ref, *, scale):
    h = h_ref[...]
    cos = cos_ref[...]
    sin = sin_ref[...]
    mq = jnp.dot(h, wmq_ref[...], preferred_element_type=F32)
    qn = _rms(mq, gq_ref[...]).astype(BF16)
    ckv = jnp.dot(h, wckv_ref[...], preferred_element_type=F32)
    cn = _rms(ckv, gkv_ref[...]).astype(BF16)
    kr = (jnp.dot(h, wkr_ref[...], preferred_element_type=F32) * cos
          + jnp.dot(h, wkrs_ref[...], preferred_element_type=F32) * sin)
    for hd in range(MLA_HEADS):
        cols = slice(hd * LANES, (hd + 1) * LANES)
        q = (jnp.dot(qn, wq_ref[:, cols], preferred_element_type=F32) * cos
             + jnp.dot(qn, wqs_ref[:, cols], preferred_element_type=F32) * sin)
        q_ref[:, cols] = (q * scale).astype(q_ref.dtype)
        k = jnp.dot(cn, wk_ref[:, cols], preferred_element_type=F32) + kr
        k_ref[:, cols] = k.astype(k_ref.dtype)
    row = lax.broadcasted_iota(jnp.int32, (MLA_HEADS * MLA_VT_ROWS, 1), 0)
    ones_col = jnp.where(row % MLA_VT_ROWS >= MLA_V, 1.0, 0.0)
    vt = (_nt_dot(wvt_ref[...], cn) + ones_col).astype(vt_ref.dtype)
    for j in range(vt_ref.shape[0]):
        vt_ref[j] = vt[:, j * TKV:(j + 1) * TKV]


def _mla_prep(h, w, cos_tab, sin_tab, seq):
    t, d = h.shape
    tpb = seq // TM
    full = lambda a: pl.BlockSpec(a.shape, lambda i: (0,) * a.ndim)
    tab = pl.BlockSpec((TM, LANES), lambda i: (i % tpb, 0))
    ws = [w["mq"], w["ckv"], w["kr"], w["krs"], w["gq"], w["gkv"], w["q"], w["qs"], w["k"], w["vt"]]
    scale = (MLA_NOPE + MLA_ROPE) ** -0.5 * LOG2E
    vt_rows = MLA_HEADS * MLA_VT_ROWS
    return pl.pallas_call(
        partial(_mla_prep_kernel, scale=scale),
        grid=(t // TM,),
        in_specs=[pl.BlockSpec((TM, d), lambda i: (i, 0))] + [full(a) for a in ws] + [tab, tab],
        out_specs=[pl.BlockSpec((TM, MLA_HEADS * LANES), lambda i: (i, 0)),
                   pl.BlockSpec((TM, MLA_HEADS * LANES), lambda i: (i, 0)),
                   pl.BlockSpec((TM // TKV, vt_rows, TKV), lambda i: (i, 0, 0))],
        out_shape=[jax.ShapeDtypeStruct((t, MLA_HEADS * LANES), BF16),
                   jax.ShapeDtypeStruct((t, MLA_HEADS * LANES), BF16),
                   jax.ShapeDtypeStruct((t // TKV, vt_rows, TKV), BF16)],
        compiler_params=_cp(1),
        name="mla_prep",
    )(h, *ws, cos_tab, sin_tab)


def _nt_dot(a, b):
    return lax.dot_general(a, b, (((1,), (1,)), ((), ())), preferred_element_type=F32)


PIPE_DEPTH = 4


def _run_pipelined(units, scores, softmax, accumulate):
    n = len(units)
    pending = {i: scores(units[i]) for i in range(min(PIPE_DEPTH, n))}
    for i in range(n):
        alpha, p = softmax(units[i], pending.pop(i))
        if i + PIPE_DEPTH < n:
            pending[i + PIPE_DEPTH] = scores(units[i + PIPE_DEPTH])
        accumulate(units[i], alpha, p)


def _half_masks(rows):
    lane = lax.broadcasted_iota(jnp.int32, (rows, LANES), 1)
    lo = (lane < LANES // 2)
    return lo, jnp.logical_not(lo)


def _masked_halves(q, scale, lo, hi):
    qf = q.astype(F32) * scale
    return jnp.where(lo, qf, 0.0).astype(BF16), jnp.where(hi, qf, 0.0).astype(BF16)


def _diff_prep_kernel(h_ref, wqk_ref, wvt_ref, qk_ref, vt_ref):
    h = h_ref[...]
    tn = 512
    for j in range(wqk_ref.shape[1] // tn):
        qk_ref[:, j * tn:(j + 1) * tn] = jnp.dot(h, wqk_ref[:, j * tn:(j + 1) * tn],
                                                 preferred_element_type=F32).astype(qk_ref.dtype)
    row = lax.broadcasted_iota(jnp.int32, (DIFF_HEADS * DIFF_VT_ROWS, 1), 0)
    ones_col = jnp.where(row % DIFF_VT_ROWS >= DIFF_V_DIM, 1.0, 0.0)
    vt = (_nt_dot(wvt_ref[...], h) + ones_col).astype(vt_ref.dtype)
    for j in range(vt_ref.shape[0]):
        vt_ref[j] = vt[:, j * TKV:(j + 1) * TKV]


def _diff_prep(h, wqk, wvt):
    t, d = h.shape
    n = wqk.shape[1]
    vt_rows = wvt.shape[0]
    return pl.pallas_call(
        _diff_prep_kernel,
        grid=(t // TM,),
        in_specs=[pl.BlockSpec((TM, d), lambda i: (i, 0)),
                  pl.BlockSpec((d, n), lambda i: (0, 0)),
                  pl.BlockSpec((vt_rows, d), lambda i: (0, 0))],
        out_specs=[pl.BlockSpec((TM, n), lambda i: (i, 0)),
                   pl.BlockSpec((TM // TKV, vt_rows, TKV), lambda i: (i, 0, 0))],
        out_shape=[jax.ShapeDtypeStruct((t, n), BF16),
                   jax.ShapeDtypeStruct((t // TKV, vt_rows, TKV), BF16)],
        compiler_params=_cp(1),
        name="diff_prep",
    )(h, wqk, wvt)


def _diff_bias_tiles():
    kj = np.arange(TKV)[:, None]
    qi = np.arange(TKV)[None, :]
    rel = (qi - kj).astype(np.float64)
    ok = (kj // CHUNK) <= (qi // CHUNK)
    off, diag = [], []
    for hd in range(DIFF_HEADS):
        c = 2.0 ** (-8.0 / DIFF_HEADS * (hd + 1)) * LOG2E
        off.append(-c * rel)
        diag.append(np.where(ok, -c * np.abs(rel), NEG))
    return np.stack([np.stack(off), np.stack(diag)]).astype(np.float32)


def _diff_attn_kernel(q_ref, k_ref, vt_ref, bias_ref, lq1_ref, lk1_ref, lq2_ref, lk2_ref, g_ref, o_ref,
                      qm_ref, m_ref, acc_ref, *, tq, lam_init):
    qi = pl.program_id(1)
    lo, hi = _half_masks(tq)
    scale = DIFF_HEAD_DIM ** -0.5 * LOG2E
    for hd in range(DIFF_HEADS):
        q0, q1 = _masked_halves(q_ref[:, hd * LANES:(hd + 1) * LANES], scale, lo, hi)
        qm_ref[2 * hd] = q0
        qm_ref[2 * hd + 1] = q1
    m_ref[...] = jnp.full(m_ref.shape, NEG, F32)
    acc_ref[...] = jnp.zeros(acc_ref.shape, F32)

    def scores(unit):
        kt, hm, diag = unit
        hd = hm // 2
        k = k_ref[pl.ds(pl.multiple_of(kt * tq, tq), tq), hd * LANES:(hd + 1) * LANES]
        return _nt_dot(k, qm_ref[hm]) + bias_ref[1 if diag else 0, hd]

    def softmax(unit, s):
        kt, hm, diag = unit
        c = 2.0 ** (-8.0 / DIFF_HEADS * (hm // 2 + 1)) * LOG2E
        shift = 0.0 if diag else (-c * tq) * (qi - kt).astype(F32)
        m_prev = m_ref[hm:hm + 1, :]
        m_new = jnp.maximum(m_prev, jnp.max(s, axis=0, keepdims=True) + shift)
        m_ref[hm:hm + 1, :] = m_new
        p = jnp.exp2(s - (m_new - shift)).astype(BF16)
        return jnp.exp2(m_prev - m_new), p

    def accumulate(unit, alpha, p):
        kt, hm, _ = unit
        hd = hm // 2
        pv = jnp.dot(vt_ref[kt, hd * DIFF_VT_ROWS:(hd + 1) * DIFF_VT_ROWS, :], p,
                     preferred_element_type=F32)
        acc_ref[hm] = alpha * acc_ref[hm] + pv

    def tiles(kts):
        _run_pipelined([(kt, hm, diag) for kt, diag in kts for hm in range(2 * DIFF_HEADS)],
                       scores, softmax, accumulate)

    def body(j, carry):
        tiles([(2 * j, False), (2 * j + 1, False)])
        return carry

    lax.fori_loop(0, qi // 2, body, 0)

    @pl.when(qi % 2 == 1)
    def _():
        tiles([(qi - 1, False), (qi, True)])

    @pl.when(qi % 2 == 0)
    def _():
        tiles([(qi, True)])

    lam = (jnp.exp(jnp.sum(lq1_ref[...] * lk1_ref[...], axis=-1, keepdims=True))
           - jnp.exp(jnp.sum(lq2_ref[...] * lk2_ref[...], axis=-1, keepdims=True)) + lam_init)
    for hd in range(DIFF_HEADS):
        a0 = acc_ref[2 * hd]
        a1 = acc_ref[2 * hd + 1]
        o = (a0[0:DIFF_V_DIM, :] * (1.0 / a0[DIFF_V_DIM:DIFF_V_DIM + 1, :])
             - lam * (a1[0:DIFF_V_DIM, :] * (1.0 / a1[DIFF_V_DIM:DIFF_V_DIM + 1, :])))
        y = o * lax.rsqrt(jnp.mean(o * o, axis=0, keepdims=True) + NORM_EPS) * g_ref[...] * (1.0 - lam_init)
        o_ref[:, hd * LANES:(hd + 1) * LANES] = y.T.astype(o_ref.dtype)


def _diff_attn(qk, vt, lq1, lk1, lq2, lk2, g, batch, seq, lam_init):
    t = qk.shape[0]
    tq = TKV
    nq = seq // tq
    bias = _diff_bias_tiles()
    vec = lambda a: pl.BlockSpec(a.shape, lambda b, i: (0, 0))
    args = [a.reshape(1, -1) for a in (lq1, lk1, lq2, lk2)] + [g.reshape(-1, 1)]
    return pl.pallas_call(
        partial(_diff_attn_kernel, tq=tq, lam_init=lam_init),
        grid=(batch, nq),
        in_specs=[pl.BlockSpec((tq, DIFF_WIDTH), lambda b, i: (b * nq + i, 0)),
                  pl.BlockSpec((seq, DIFF_WIDTH), lambda b, i: (b, 1)),
                  pl.BlockSpec((nq,) + vt.shape[1:], lambda b, i: (b, 0, 0)),
                  pl.BlockSpec(bias.shape, lambda b, i: (0, 0, 0, 0))] + [vec(a) for a in args],
        out_specs=pl.BlockSpec((tq, DIFF_WIDTH), lambda b, i: (b * nq + i, 0)),
        out_shape=jax.ShapeDtypeStruct((t, DIFF_WIDTH), BF16),
        scratch_shapes=[pltpu.VMEM((2 * DIFF_HEADS, tq, LANES), BF16),
                        pltpu.VMEM((2 * DIFF_HEADS, tq), F32),
                        pltpu.VMEM((2 * DIFF_HEADS, DIFF_VT_ROWS, tq), F32)],
        compiler_params=_cp(2),
        name="diff_attn",
    )(qk, qk, vt, bias, *args)


def _chunk_attn_kernel(q_ref, k_ref, v_ref, bias_ref, o_ref, ks_ref, vs_ref, *, seq):
    qi = pl.program_id(1)

    @pl.when(qi == 0)
    def _():
        ks_ref[0:CHK_PAD, :] = jnp.zeros((CHK_PAD, CHK_WIDTH), ks_ref.dtype)
        vs_ref[0:CHK_PAD, :] = jnp.zeros((CHK_PAD, CHK_WIDTH), vs_ref.dtype)
        ks_ref[CHK_PAD:CHK_PAD + seq, :] = k_ref[...]
        vs_ref[CHK_PAD:CHK_PAD + seq, :] = v_ref[...]

    q0 = pl.multiple_of(qi * TQ_CHK, TQ_CHK)
    scale = CHK_HEAD_DIM ** -0.5
    lo, hi = _half_masks(TQ_CHK)
    col = lax.broadcasted_iota(jnp.int32, (TQ_CHK, CHK_WIN), 1)
    in_seq = col >= (CHK_PAD - q0)
    for pair in range(CHK_HEADS // 2):
        cols = slice(pair * LANES, (pair + 1) * LANES)
        q_halves = _masked_halves(q_ref[:, cols], scale, lo, hi)
        kp = ks_ref[pl.ds(q0, CHK_WIN), cols]
        vp = vs_ref[pl.ds(q0, CHK_WIN), cols]
        outs = []
        for half in range(2):
            s = _nt_dot(q_halves[half], kp) + bias_ref[2 * pair + half]
            s = jnp.where(in_seq, s, NEG)
            m = jnp.max(s, axis=-1, keepdims=True)
            p = jnp.exp(s - m)
            l = jnp.sum(p, axis=-1, keepdims=True)
            outs.append(jnp.dot(p.astype(vp.dtype), vp, preferred_element_type=F32) / l)
        o_ref[:, cols] = jnp.where(lo, outs[0], outs[1]).astype(o_ref.dtype)


def _chunk_attn(qkv, bias, batch, seq):
    t = qkv.shape[0]
    nq = seq // TQ_CHK
    return pl.pallas_call(
        partial(_chunk_attn_kernel, seq=seq),
        grid=(batch, nq),
        in_specs=[pl.BlockSpec((TQ_CHK, CHK_WIDTH), lambda b, i: (b * nq + i, 0)),
                  pl.BlockSpec((seq, CHK_WIDTH), lambda b, i: (b, 1)),
                  pl.BlockSpec((seq, CHK_WIDTH), lambda b, i: (b, 2)),
                  pl.BlockSpec(bias.shape, lambda b, i: (0, 0, 0))],
        out_specs=pl.BlockSpec((TQ_CHK, CHK_WIDTH), lambda b, i: (b * nq + i, 0)),
        out_shape=jax.ShapeDtypeStruct((t, CHK_WIDTH), BF16),
        scratch_shapes=[pltpu.VMEM((CHK_PAD + seq, CHK_WIDTH), BF16),
                        pltpu.VMEM((CHK_PAD + seq, CHK_WIDTH), BF16)],
        compiler_params=_cp(2),
        name="chunk_attn",
    )(qkv, qkv, qkv, bias)


def _chunk_bias(rel_table):
    qi = np.arange(TQ_CHK)[:, None]
    kj = np.arange(CHK_WIN)[None, :]
    dchunk = (qi + CHK_PAD) // CHUNK - kj // CHUNK
    band = (dchunk >= 0) & (dchunk <= CHK_LEFT_CHUNKS)
    n_clipped = CHK_WIN - 1 - REL_CLIP
    lo = 2 * REL_CLIP + 1 - (TQ_CHK + CHK_WIN - 1 - n_clipped)
    seq = jnp.concatenate([jnp.broadcast_to(rel_table[:, 2 * REL_CLIP:], (rel_table.shape[0], n_clipped)),
                           rel_table[:, lo:][:, ::-1]], axis=1)
    bias = jnp.stack([seq[:, TQ_CHK - 1 - i:TQ_CHK - 1 - i + CHK_WIN] for i in range(TQ_CHK)], axis=1)
    return jnp.where(jnp.asarray(band)[None], bias.astype(F32), NEG)


def _mla_attn_kernel(q_ref, k_ref, vt_ref, o_ref, m_ref, acc_ref, *, tq):
    qi = pl.program_id(1)
    m_ref[...] = jnp.full(m_ref.shape, NEG, F32)
    acc_ref[...] = jnp.zeros(acc_ref.shape, F32)
    ki = lax.broadcasted_iota(jnp.int32, (tq, tq), 0)
    qj = lax.broadcasted_iota(jnp.int32, (tq, tq), 1)
    diag_ok = (ki // CHUNK) <= (qj // CHUNK)

    def scores(unit):
        kt, hd, masked = unit
        cols = slice(hd * LANES, (hd + 1) * LANES)
        s = _nt_dot(k_ref[pl.ds(pl.multiple_of(kt * tq, tq), tq), cols], q_ref[:, cols])
        return jnp.where(diag_ok, s, NEG) if masked else s

    def softmax_pv(unit, s):
        kt, hd, _ = unit
        m_prev = m_ref[hd:hd + 1, :]
        m_new = jnp.maximum(m_prev, jnp.max(s, axis=0, keepdims=True))
        m_ref[hd:hd + 1, :] = m_new
        p = jnp.exp2(s - m_new).astype(BF16)
        return jnp.exp2(m_prev - m_new), p

    def accumulate(unit, alpha, p):
        kt, hd, _ = unit
        pv = jnp.dot(vt_ref[kt, hd * MLA_VT_ROWS:(hd + 1) * MLA_VT_ROWS, :], p,
                     preferred_element_type=F32)
        acc_ref[hd] = alpha * acc_ref[hd] + pv

    def tiles(kts):
        _run_pipelined([(kt, hd, masked) for kt, masked in kts for hd in range(MLA_HEADS)],
                       scores, softmax_pv, accumulate)

    def body(j, carry):
        tiles([(2 * j, False), (2 * j + 1, False)])
        return carry

    lax.fori_loop(0, qi // 2, body, 0)

    @pl.when(qi % 2 == 1)
    def _():
        tiles([(qi - 1, False), (qi, True)])

    @pl.when(qi % 2 == 0)
    def _():
        tiles([(qi, True)])

    for pair in range(MLA_HEADS // 2):
        halves = []
        for hd in (2 * pair, 2 * pair + 1):
            acc = acc_ref[hd]
            halves.append(acc[0:MLA_V, :] * (1.0 / acc[MLA_V:MLA_V + 1, :]))
        o_ref[:, pair * LANES:(pair + 1) * LANES] = jnp.concatenate(halves, axis=0).T.astype(o_ref.dtype)


def _mla_attn(q, k, vt, batch, seq):
    t = q.shape[0]
    tq = TKV
    nq = seq // tq
    return pl.pallas_call(
        partial(_mla_attn_kernel, tq=tq),
        grid=(batch, nq),
        in_specs=[pl.BlockSpec((tq, q.shape[1]), lambda b, i: (b * nq + i, 0)),
                  pl.BlockSpec((seq, k.shape[1]), lambda b, i: (b, 0)),
                  pl.BlockSpec((nq,) + vt.shape[1:], lambda b, i: (b, 0, 0))],
        out_specs=pl.BlockSpec((tq, MLA_HEADS * MLA_V), lambda b, i: (b * nq + i, 0)),
        out_shape=jax.ShapeDtypeStruct((t, MLA_HEADS * MLA_V), BF16),
        scratch_shapes=[pltpu.VMEM((MLA_HEADS, tq), F32),
                        pltpu.VMEM((MLA_HEADS, MLA_VT_ROWS, tq), F32)],
        compiler_params=_cp(2),
        name="mla_attn",
    )(q, k, vt)


def _route(logits_t, bias_ref):
    e_rows = [logits_t[e:e + 1, :] for e in range(N_EXPERTS)]
    scores = [jax.nn.sigmoid(r) for r in e_rows]
    sel = [scores[e] + bias_ref[e] for e in range(N_EXPERTS)]
    gscore = []
    for g in range(N_GROUPS):
        s4 = sel[g * EXPERTS_PER_GROUP:(g + 1) * EXPERTS_PER_GROUP]
        best = None
        for a in range(EXPERTS_PER_GROUP):
            for b in range(a + 1, EXPERTS_PER_GROUP):
                pair = s4[a] + s4[b]
                best = pair if best is None else jnp.maximum(best, pair)
        gscore.append(best)
    gbest = gscore[0]
    gidx = jnp.zeros_like(gbest, dtype=jnp.int32)
    for g in range(1, N_GROUPS):
        better = gscore[g] > gbest
        gbest = jnp.where(better, gscore[g], gbest)
        gidx = jnp.where(better, g, gidx)
    masked = [jnp.where(gidx == e // EXPERTS_PER_GROUP, sel[e], -jnp.inf) for e in range(N_EXPERTS)]

    def arg_first_max(vals):
        top = vals[0]
        for v in vals[1:]:
            top = jnp.maximum(top, v)
        idx = jnp.full(top.shape, N_EXPERTS, jnp.int32)
        for e in range(N_EXPERTS - 1, -1, -1):
            idx = jnp.where(vals[e] == top, e, idx)
        return idx

    i0 = arg_first_max(masked)
    i1 = arg_first_max([jnp.where(i0 == e, -jnp.inf, masked[e]) for e in range(N_EXPERTS)])
    w0 = sum(jnp.where(i0 == e, scores[e], 0.0) for e in range(N_EXPERTS))
    w1 = sum(jnp.where(i1 == e, scores[e], 0.0) for e in range(N_EXPERTS))
    den = w0 + w1
    w0 = w0 / den
    w1 = w1 / den
    rows = [jnp.where(i0 == e, w0, 0.0) + jnp.where(i1 == e, w1, 0.0) for e in range(N_EXPERTS)]
    return jnp.concatenate(rows, axis=0)


def _merge_kernel(ya_ref, yb_ref, yc_ref, gate_ref, x_ref, gt_ref, sc_ref, sh_ref, g_ref,
                  wa_ref, wb_ref, wc_ref, wo_ref, wr_ref, rb_ref,
                  xo_ref, h_ref, comb_ref):
    d = D_MODEL
    merged = (gate_ref[:, 0:d].astype(F32) * jnp.dot(ya_ref[...], wa_ref[...], preferred_element_type=F32)
              + gate_ref[:, d:2 * d].astype(F32) * jnp.dot(yb_ref[...], wb_ref[...], preferred_element_type=F32)
              + gate_ref[:, 2 * d:3 * d].astype(F32) * jnp.dot(yc_ref[...], wc_ref[...], preferred_element_type=F32))
    y = jnp.dot(merged.astype(BF16), wo_ref[...], preferred_element_type=F32)
    x_new = x_ref[...] + gt_ref[0] * y
    xo_ref[...] = x_new
    h2 = _rms(x_new, g_ref[...]) * (1.0 + sc_ref[0]) + sh_ref[0]
    h_ref[...] = h2.astype(h_ref.dtype)
    logits = jnp.dot(h2, wr_ref[...], preferred_element_type=F32, precision=lax.Precision.HIGHEST)
    comb_ref[...] = _route(logits.T[0:N_EXPERTS, :], rb_ref)


def _merge(ya, yb, yc, gates, x2, gt, sc, sh, g, w, wr_pad, rbias, seq):
    t, d = x2.shape
    tm = 256
    tpb = seq // tm
    row = lambda n: pl.BlockSpec((tm, n), lambda i: (i, 0))
    full = lambda a: pl.BlockSpec(a.shape, lambda i: (0,) * a.ndim)
    bspec = pl.BlockSpec((1, 1, d), lambda i: (i // tpb, 0, 0))
    ws = [w["br_diff"], w["br_chunk"], w["br_mla"], w["out"], wr_pad]
    g2 = g.reshape(1, d)
    return pl.pallas_call(
        _merge_kernel,
        grid=(t // tm,),
        in_specs=[row(ya.shape[1]), row(yb.shape[1]), row(yc.shape[1]), row(gates.shape[1]), row(d),
                  bspec, bspec, bspec, full(g2)] + [full(a) for a in ws]
                 + [pl.BlockSpec(memory_space=pltpu.SMEM)],
        out_specs=[row(d), row(d), pl.BlockSpec((N_EXPERTS, tm), lambda i: (0, i))],
        out_shape=[jax.ShapeDtypeStruct((t, d), F32), jax.ShapeDtypeStruct((t, d), BF16),
                   jax.ShapeDtypeStruct((N_EXPERTS, t), F32)],
        compiler_params=_cp(1),
        name="merge_route",
    )(ya, yb, yc, gates, x2, gt, sc, sh, g2, *ws, rbias)


def _moe_kernel(h_ref, wg_ref, wu_ref, wd_ref, comb_ref, x_ref, gt_ref, o_ref, acc_ref):
    e = pl.program_id(1)

    @pl.when(e == 0)
    def _():
        acc_ref[...] = jnp.zeros_like(acc_ref)

    h = h_ref[...]
    a = jnp.dot(h, wg_ref[0], preferred_element_type=F32)
    b = jnp.dot(h, wu_ref[0], preferred_element_type=F32)
    he = (a * jax.nn.sigmoid(a) * b).astype(BF16)
    acc_ref[...] += comb_ref[0] * jnp.dot(he, wd_ref[0], preferred_element_type=F32)

    @pl.when(e == pl.num_programs(1) - 1)
    def _():
        o_ref[...] = x_ref[...] + gt_ref[0] * acc_ref[...]


def _moe_dense(h2, comb, x2, gt, wg, wu, wd, seq):
    t, d = x2.shape
    tm = TM_MOE
    tpb = seq // tm
    n_e, _, dff = wg.shape
    return pl.pallas_call(
        _moe_kernel,
        grid=(t // tm, n_e),
        in_specs=[pl.BlockSpec((tm, d), lambda i, e: (i, 0)),
                  pl.BlockSpec((1, d, dff), lambda i, e: (e, 0, 0)),
                  pl.BlockSpec((1, d, dff), lambda i, e: (e, 0, 0)),
                  pl.BlockSpec((1, dff, d), lambda i, e: (e, 0, 0)),
                  pl.BlockSpec((1, tm, 1), lambda i, e: (e, i, 0)),
                  pl.BlockSpec((tm, d), lambda i, e: (i, 0)),
                  pl.BlockSpec((1, 1, d), lambda i, e: (i // tpb, 0, 0))],
        out_specs=pl.BlockSpec((tm, d), lambda i, e: (i, 0)),
        out_shape=jax.ShapeDtypeStruct((t, d), F32),
        scratch_shapes=[pltpu.VMEM((tm, d), F32)],
        compiler_params=_cp(2),
        name="moe_dense",
    )(h2, wg, wu, wd, comb, x2, gt)


def _final_norm_kernel(x_ref, g_ref, o_ref):
    o_ref[...] = _rms(x_ref[...], g_ref[...])


def _final_norm(x2, g):
    t, d = x2.shape
    return pl.pallas_call(
        _final_norm_kernel,
        grid=(t // TM,),
        in_specs=[pl.BlockSpec((TM, d), lambda i: (i, 0)), pl.BlockSpec((1, d), lambda i: (0, 0))],
        out_specs=pl.BlockSpec((TM, d), lambda i: (i, 0)),
        out_shape=jax.ShapeDtypeStruct((t, d), F32),
        compiler_params=_cp(1),
        name="final_norm",
    )(x2, g.reshape(1, d))


def _pad_heads(w, n_heads, width, start, stop, at=0):
    rows = w.shape[0]
    wh = w.reshape(rows, n_heads, width)[:, :, start:stop]
    out = jnp.zeros((rows, n_heads, LANES), w.dtype)
    out = out.at[:, :, at:at + (stop - start)].set(wh)
    return out.reshape(rows, n_heads * LANES)


def _rope_swap(w_rope):
    half = MLA_ROPE // 2
    return jnp.concatenate([-w_rope[..., half:], w_rope[..., :half]], axis=-1)


def _layer_weights(l, w_in, mla_q_norm_g, mla_w_q_b, mla_kv_norm_g, mla_w_kv_b,
                   w_branch_diff, w_branch_chunk, w_branch_mla, w_out):
    wi = w_in[l]
    o_chunk = 3 * DIFF_WIDTH
    o_mq = o_chunk + 3 * CHK_WIDTH
    o_mkv = o_mq + MLA_Q_LORA
    o_gate = o_mkv + MLA_KV_LORA + MLA_ROPE
    qk = MLA_NOPE + MLA_ROPE
    wqb = mla_w_q_b[l]
    wq_rope = wqb.reshape(MLA_Q_LORA, MLA_HEADS, qk)[:, :, MLA_NOPE:]
    wqs = jnp.zeros((MLA_Q_LORA, MLA_HEADS, LANES), F32).at[:, :, MLA_NOPE:qk].set(_rope_swap(wq_rope))
    wkr = wi[:, o_mkv + MLA_KV_LORA:o_gate]
    pad_kr = lambda a: jnp.zeros((D_MODEL, LANES), F32).at[:, MLA_NOPE:qk].set(a)
    wkvb = mla_w_kv_b[l]
    bf = lambda a: a.astype(BF16)
    return {
        "diff_qk": bf(wi[:, :2 * DIFF_WIDTH]),
        "diff_vt": bf(jnp.pad(wi[:, 2 * DIFF_WIDTH:o_chunk].reshape(D_MODEL, DIFF_HEADS, DIFF_V_DIM),
                              ((0, 0), (0, 0), (0, ONES_ROWS))).reshape(D_MODEL, DIFF_HEADS * DIFF_VT_ROWS).T),
        "chunk": bf(wi[:, o_chunk:o_mq]),
        "gate": bf(wi[:, o_gate:]),
        "mq": bf(wi[:, o_mq:o_mkv]),
        "ckv": bf(wi[:, o_mkv:o_mkv + MLA_KV_LORA]),
        "kr": bf(pad_kr(wkr)),
        "krs": bf(pad_kr(_rope_swap(wkr))),
        "gq": mla_q_norm_g[l].reshape(1, -1),
        "gkv": mla_kv_norm_g[l].reshape(1, -1),
        "q": bf(_pad_heads(wqb, MLA_HEADS, qk, 0, qk)),
        "qs": bf(wqs.reshape(MLA_Q_LORA, MLA_HEADS * LANES)),
        "k": bf(_pad_heads(wkvb, MLA_HEADS, MLA_NOPE + MLA_V, 0, MLA_NOPE)),
        "vt": bf(jnp.pad(wkvb.reshape(MLA_KV_LORA, MLA_HEADS, MLA_NOPE + MLA_V)[:, :, MLA_NOPE:],
                         ((0, 0), (0, 0), (0, ONES_ROWS))).reshape(MLA_KV_LORA, MLA_HEADS * MLA_VT_ROWS).T),
        "br_diff": bf(w_branch_diff[l]),
        "br_chunk": bf(w_branch_chunk[l]),
        "br_mla": bf(w_branch_mla[l]),
        "out": bf(w_out[l]),
    }


def _rope_tables(seq):
    pos = jnp.arange(seq, dtype=F32)
    inv_freq = 1.0 / (ROPE_THETA ** (jnp.arange(0, MLA_ROPE, 2, dtype=F32) / MLA_ROPE))
    ang = pos[:, None] * inv_freq[None, :]
    cos, sin = jnp.cos(ang), jnp.sin(ang)
    qk = MLA_NOPE + MLA_ROPE
    cos_tab = jnp.zeros((seq, LANES), F32).at[:, :MLA_NOPE].set(1.0)
    cos_tab = cos_tab.at[:, MLA_NOPE:qk].set(jnp.concatenate([cos, cos], axis=-1))
    sin_tab = jnp.zeros((seq, LANES), F32).at[:, MLA_NOPE:qk].set(jnp.concatenate([sin, sin], axis=-1))
    return cos_tab, sin_tab


def kernel(x, c, w_mod, b_mod, g_norm_mix, g_norm_ffn, w_in, diff_lambda_q1, diff_lambda_k1, diff_lambda_q2, diff_lambda_k2, diff_subln_g, chunk_rel_bias, mla_q_norm_g, mla_w_q_b, mla_kv_norm_g, mla_w_kv_b, w_branch_diff, w_branch_chunk, w_branch_mla, w_out, w_router, router_bias, w_exp_gate, w_exp_up, w_exp_down, g_final):
    batch, seq, d = x.shape
    depth = w_in.shape[0]
    t = batch * seq
    x2 = x.reshape(t, d)
    cos_tab, sin_tab = _rope_tables(seq)
    mod = _modulation(c, w_mod, b_mod)
    wr_pad = jnp.zeros((d, LANES), F32).at[:, :N_EXPERTS].set(w_router)
    for l in range(depth):
        sh1, sc1, gt1, sh2, sc2, gt2 = [m.reshape(batch, 1, d) for m in jnp.split(mod[l], 6, axis=-1)]
        w = _layer_weights(l, w_in, mla_q_norm_g, mla_w_q_b, mla_kv_norm_g, mla_w_kv_b,
                           w_branch_diff, w_branch_chunk, w_branch_mla, w_out)
        h = _norm_mod(x2, g_norm_mix[l], sc1, sh1, seq)
        lam_init = 0.8 - 0.6 * math.exp(-0.3 * l)
        dqk, dvt = _diff_prep(h, w["diff_qk"], w["diff_vt"])
        ya = _diff_attn(dqk, dvt, diff_lambda_q1[l], diff_lambda_k1[l], diff_lambda_q2[l], diff_lambda_k2[l],
                        diff_subln_g[l], batch, seq, lam_init)
        yb = _chunk_attn(_proj(h, w["chunk"], name="proj_chunk"), _chunk_bias(chunk_rel_bias[l]), batch, seq)
        q, k, v = _mla_prep(h, w, cos_tab, sin_tab, seq)
        yc = _mla_attn(q, k, v, batch, seq)
        gates = _proj(h, w["gate"], sigmoid=True, name="proj_gate")
        x2, h2, comb_t = _merge(ya, yb, yc, gates, x2, gt1, sc2, sh2, g_norm_ffn[l], w, wr_pad,
                                router_bias, seq)
        comb = comb_t.reshape(N_EXPERTS, t, 1)
        x2 = _moe_dense(h2, comb, x2, gt2, w_exp_gate[l].astype(BF16), w_exp_up[l].astype(BF16),
                        w_exp_down[l].astype(BF16), seq)
    return _final_norm(x2, g_final).reshape(batch, seq, d)
```

```python
import math
from functools import partial

import jax
import jax.numpy as jnp
import numpy as np
from jax import lax
from jax.experimental import pallas as pl
from jax.experimental.pallas import tpu as pltpu

F32 = jnp.float32
BF16 = jnp.bfloat16

D_MODEL = 1024
CHUNK = 64
NORM_EPS = 1e-6
DIFF_HEADS = 4
DIFF_HEAD_DIM = 64
DIFF_V_DIM = 128
DIFF_WIDTH = 512
CHK_HEADS = 8
CHK_HEAD_DIM = 64
CHK_WIDTH = 512
CHK_LEFT_CHUNKS = 8
REL_CLIP = 256
MLA_HEADS = 8
MLA_Q_LORA = 384
MLA_KV_LORA = 256
MLA_NOPE = 64
MLA_ROPE = 32
MLA_V = 64
ROPE_THETA = 10000.0
N_EXPERTS = 16
N_GROUPS = 4
EXPERTS_PER_GROUP = 4
MOE_D_FF = 512

LANES = 128
NEG = -1e30
VMEM_LIMIT = 56 * 1024 * 1024

TM = 512
TKV = 256
ONES_ROWS = 16
MLA_VT_ROWS = MLA_V + ONES_ROWS
DIFF_VT_ROWS = DIFF_V_DIM + ONES_ROWS
LOG2E = math.log2(math.e)
CHK_VT_ROWS = CHK_HEAD_DIM + ONES_ROWS
CHK_WIN_TILES = CHK_LEFT_CHUNKS * CHUNK // TKV + 1
TM_MOE = 1024


def _cp(n_axes):
    return pltpu.CompilerParams(dimension_semantics=("arbitrary",) * n_axes,
                                vmem_limit_bytes=VMEM_LIMIT)


def _rms(x, g):
    return x * lax.rsqrt(jnp.mean(x * x, axis=-1, keepdims=True) + NORM_EPS) * g


def _mod_kernel(c_ref, w_ref, b_ref, o_ref):
    c = c_ref[...]
    c_act = c * jax.nn.sigmoid(c)
    o_ref[0] = jnp.dot(c_act.astype(BF16), w_ref[0].astype(BF16),
                       preferred_element_type=F32) + b_ref[0]


def _modulation(c, w_mod, b_mod):
    n_layers, d, n = w_mod.shape
    b = c.shape[0]
    tn = 1024
    return pl.pallas_call(
        _mod_kernel,
        grid=(n_layers, n // tn),
        in_specs=[pl.BlockSpec((b, d), lambda l, j: (0, 0)),
                  pl.BlockSpec((1, d, tn), lambda l, j: (l, 0, j)),
                  pl.BlockSpec((1, 1, tn), lambda l, j: (l, 0, j))],
        out_specs=pl.BlockSpec((1, b, tn), lambda l, j: (l, 0, j)),
        out_shape=jax.ShapeDtypeStruct((n_layers, b, n), F32),
        compiler_params=_cp(2),
        name="adaln_mod",
    )(c, w_mod, b_mod.reshape(n_layers, 1, n))


def _norm_mod_kernel(x_ref, g_ref, sc_ref, sh_ref, o_ref):
    h = _rms(x_ref[...], g_ref[...]) * (1.0 + sc_ref[0]) + sh_ref[0]
    o_ref[...] = h.astype(o_ref.dtype)


def _norm_mod(x2, g, sc, sh, seq):
    t, d = x2.shape
    tpb = seq // TM
    bspec = pl.BlockSpec((1, 1, d), lambda i: (i // tpb, 0, 0))
    return pl.pallas_call(
        _norm_mod_kernel,
        grid=(t // TM,),
        in_specs=[pl.BlockSpec((TM, d), lambda i: (i, 0)),
                  pl.BlockSpec((1, d), lambda i: (0, 0)), bspec, bspec],
        out_specs=pl.BlockSpec((TM, d), lambda i: (i, 0)),
        out_shape=jax.ShapeDtypeStruct((t, d), BF16),
        compiler_params=_cp(1),
        name="norm_mod",
    )(x2, g.reshape(1, d), sc, sh)


def _proj_kernel(h_ref, w_ref, o_ref, *, tn, sigmoid):
    h = h_ref[...]
    for j in range(w_ref.shape[1] // tn):
        r = jnp.dot(h, w_ref[:, j * tn:(j + 1) * tn], preferred_element_type=F32)
        if sigmoid:
            r = jax.nn.sigmoid(r)
        o_ref[:, j * tn:(j + 1) * tn] = r.astype(o_ref.dtype)


def _proj(h, w, sigmoid=False, name="proj"):
    t, d = h.shape
    n = w.shape[1]
    return pl.pallas_call(
        partial(_proj_kernel, tn=512, sigmoid=sigmoid),
        grid=(t // TM,),
        in_specs=[pl.BlockSpec((TM, d), lambda i: (i, 0)),
                  pl.BlockSpec((d, n), lambda i: (0, 0))],
        out_specs=pl.BlockSpec((TM, n), lambda i: (i, 0)),
        out_shape=jax.ShapeDtypeStruct((t, n), BF16),
        compiler_params=_cp(1),
        name=name,
    )(h, w)


def _mla_prep_kernel(h_ref, wmq_ref, wckv_ref, wkr_ref, wkrs_ref, gq_ref, gkv_ref,
                     wq_ref, wqs_ref, wk_ref, wvt_ref, cos_ref, sin_ref,
                     q_ref, k_ref, vt_ref, *, scale):
    h = h_ref[...]
    cos = cos_ref[...]
    sin = sin_ref[...]
    mq = jnp.dot(h, wmq_ref[...], preferred_element_type=F32)
    qn = _rms(mq, gq_ref[...]).astype(BF16)
    ckv = jnp.dot(h, wckv_ref[...], preferred_element_type=F32)
    cn = _rms(ckv, gkv_ref[...]).astype(BF16)
    kr = (jnp.dot(h, wkr_ref[...], preferred_element_type=F32) * cos
          + jnp.dot(h, wkrs_ref[...], preferred_element_type=F32) * sin)
    for hd in range(MLA_HEADS):
        cols = slice(hd * LANES, (hd + 1) * LANES)
        q = (jnp.dot(qn, wq_ref[:, cols], preferred_element_type=F32) * cos
             + jnp.dot(qn, wqs_ref[:, cols], preferred_element_type=F32) * sin)
        q_ref[:, cols] = (q * scale).astype(q_ref.dtype)
        k = jnp.dot(cn, wk_ref[:, cols], preferred_element_type=F32) + kr
        k_ref[:, cols] = k.astype(k_ref.dtype)
    row = lax.broadcasted_iota(jnp.int32, (MLA_HEADS * MLA_VT_ROWS, 1), 0)
    ones_col = jnp.where(row % MLA_VT_ROWS >= MLA_V, 1.0, 0.0)
    vt = (_nt_dot(wvt_ref[...], cn) + ones_col).astype(vt_ref.dtype)
    for j in range(vt_ref.shape[0]):
        vt_ref[j] = vt[:, j * TKV:(j + 1) * TKV]


def _mla_prep(h, w, cos_tab, sin_tab, seq):
    t, d = h.shape
    tpb = seq // TM
    full = lambda a: pl.BlockSpec(a.shape, lambda i: (0,) * a.ndim)
    tab = pl.BlockSpec((TM, LANES), lambda i: (i % tpb, 0))
    ws = [w["mq"], w["ckv"], w["kr"], w["krs"], w["gq"], w["gkv"], w["q"], w["qs"], w["k"], w["vt"]]
    scale = (MLA_NOPE + MLA_ROPE) ** -0.5 * LOG2E
    vt_rows = MLA_HEADS * MLA_VT_ROWS
    return pl.pallas_call(
        partial(_mla_prep_kernel, scale=scale),
        grid=(t // TM,),
        in_specs=[pl.BlockSpec((TM, d), lambda i: (i, 0))] + [full(a) for a in ws] + [tab, tab],
        out_specs=[pl.BlockSpec((TM, MLA_HEADS * LANES), lambda i: (i, 0)),
                   pl.BlockSpec((TM, MLA_HEADS * LANES), lambda i: (i, 0)),
                   pl.BlockSpec((TM // TKV, vt_rows, TKV), lambda i: (i, 0, 0))],
        out_shape=[jax.ShapeDtypeStruct((t, MLA_HEADS * LANES), BF16),
                   jax.ShapeDtypeStruct((t, MLA_HEADS * LANES), BF16),
                   jax.ShapeDtypeStruct((t // TKV, vt_rows, TKV), BF16)],
        compiler_params=_cp(1),
        name="mla_prep",
    )(h, *ws, cos_tab, sin_tab)


def _nt_dot(a, b):
    return lax.dot_general(a, b, (((1,), (1,)), ((), ())), preferred_element_type=F32)


PIPE_DEPTH = 4


def _run_pipelined(units, scores, softmax, accumulate):
    n = len(units)
    pending = {i: scores(units[i]) for i in range(min(PIPE_DEPTH, n))}
    for i in range(n):
        alpha, p = softmax(units[i], pending.pop(i))
        if i + PIPE_DEPTH < n:
            pending[i + PIPE_DEPTH] = scores(units[i + PIPE_DEPTH])
        accumulate(units[i], alpha, p)


def _half_masks(rows):
    lane = lax.broadcasted_iota(jnp.int32, (rows, LANES), 1)
    lo = (lane < LANES // 2)
    return lo, jnp.logical_not(lo)


def _masked_halves(q, scale, lo, hi):
    qf = q.astype(F32) * scale
    return jnp.where(lo, qf, 0.0).astype(BF16), jnp.where(hi, qf, 0.0).astype(BF16)


def _qk_vt_kernel(h_ref, wqk_ref, wvt_ref, qk_ref, vt_ref, *, v_dim):
    h = h_ref[...]
    tn = 512
    for j in range(wqk_ref.shape[1] // tn):
        qk_ref[:, j * tn:(j + 1) * tn] = jnp.dot(h, wqk_ref[:, j * tn:(j + 1) * tn],
                                                 preferred_element_type=F32).astype(qk_ref.dtype)
    row = lax.broadcasted_iota(jnp.int32, (wvt_ref.shape[0], 1), 0)
    ones_col = jnp.where(row % (v_dim + ONES_ROWS) >= v_dim, 1.0, 0.0)
    vt = (_nt_dot(wvt_ref[...], h) + ones_col).astype(vt_ref.dtype)
    for j in range(vt_ref.shape[0]):
        vt_ref[j] = vt[:, j * TKV:(j + 1) * TKV]


def _qk_vt(h, wqk, wvt, v_dim, name):
    t, d = h.shape
    n = wqk.shape[1]
    vt_rows = wvt.shape[0]
    return pl.pallas_call(
        partial(_qk_vt_kernel, v_dim=v_dim),
        grid=(t // TM,),
        in_specs=[pl.BlockSpec((TM, d), lambda i: (i, 0)),
                  pl.BlockSpec((d, n), lambda i: (0, 0)),
                  pl.BlockSpec((vt_rows, d), lambda i: (0, 0))],
        out_specs=[pl.BlockSpec((TM, n), lambda i: (i, 0)),
                   pl.BlockSpec((TM // TKV, vt_rows, TKV), lambda i: (i, 0, 0))],
        out_shape=[jax.ShapeDtypeStruct((t, n), BF16),
                   jax.ShapeDtypeStruct((t // TKV, vt_rows, TKV), BF16)],
        compiler_params=_cp(1),
        name=name,
    )(h, wqk, wvt)


def _diff_bias_tiles():
    kj = np.arange(TKV)[:, None]
    qi = np.arange(TKV)[None, :]
    rel = (qi - kj).astype(np.float64)
    ok = (kj // CHUNK) <= (qi // CHUNK)
    off, diag = [], []
    for hd in range(DIFF_HEADS):
        c = 2.0 ** (-8.0 / DIFF_HEADS * (hd + 1)) * LOG2E
        off.append(-c * rel)
        diag.append(np.where(ok, -c * np.abs(rel), NEG))
    return np.stack([np.stack(off), np.stack(diag)]).astype(np.float32)


def _diff_attn_kernel(q_ref, k_ref, vt_ref, bias_ref, lq1_ref, lk1_ref, lq2_ref, lk2_ref, g_ref, o_ref,
                      qm_ref, m_ref, acc_ref, *, tq, lam_init):
    qi = pl.program_id(1)
    lo, hi = _half_masks(tq)
    scale = DIFF_HEAD_DIM ** -0.5 * LOG2E
    for hd in range(DIFF_HEADS):
        q0, q1 = _masked_halves(q_ref[:, hd * LANES:(hd + 1) * LANES], scale, lo, hi)
        qm_ref[2 * hd] = q0
        qm_ref[2 * hd + 1] = q1
    m_ref[...] = jnp.full(m_ref.shape, NEG, F32)
    acc_ref[...] = jnp.zeros(acc_ref.shape, F32)

    def scores(unit):
        kt, hm, diag = unit
        hd = hm // 2
        k = k_ref[pl.ds(pl.multiple_of(kt * tq, tq), tq), hd * LANES:(hd + 1) * LANES]
        return _nt_dot(k, qm_ref[hm]) + bias_ref[1 if diag else 0, hd]

    def softmax(unit, s):
        kt, hm, diag = unit
        c = 2.0 ** (-8.0 / DIFF_HEADS * (hm // 2 + 1)) * LOG2E
        shift = 0.0 if diag else (-c * tq) * (qi - kt).astype(F32)
        m_prev = m_ref[hm:hm + 1, :]
        m_new = jnp.maximum(m_prev, jnp.max(s, axis=0, keepdims=True) + shift)
        m_ref[hm:hm + 1, :] = m_new
        p = jnp.exp2(s - (m_new - shift)).astype(BF16)
        return jnp.exp2(m_prev - m_new), p

    def accumulate(unit, alpha, p):
        kt, hm, _ = unit
        hd = hm // 2
        pv = jnp.dot(vt_ref[kt, hd * DIFF_VT_ROWS:(hd + 1) * DIFF_VT_ROWS, :], p,
                     preferred_element_type=F32)
        acc_ref[hm] = alpha * acc_ref[hm] + pv

    def tiles(kts):
        _run_pipelined([(kt, hm, diag) for kt, diag in kts for hm in range(2 * DIFF_HEADS)],
                       scores, softmax, accumulate)

    def body(j, carry):
        tiles([(2 * j, False), (2 * j + 1, False)])
        return carry

    lax.fori_loop(0, qi // 2, body, 0)

    @pl.when(qi % 2 == 1)
    def _():
        tiles([(qi - 1, False), (qi, True)])

    @pl.when(qi % 2 == 0)
    def _():
        tiles([(qi, True)])

    lam = (jnp.exp(jnp.sum(lq1_ref[...] * lk1_ref[...], axis=-1, keepdims=True))
           - jnp.exp(jnp.sum(lq2_ref[...] * lk2_ref[...], axis=-1, keepdims=True)) + lam_init)
    for hd in range(DIFF_HEADS):
        a0 = acc_ref[2 * hd]
        a1 = acc_ref[2 * hd + 1]
        o = (a0[0:DIFF_V_DIM, :] * (1.0 / a0[DIFF_V_DIM:DIFF_V_DIM + 1, :])
             - lam * (a1[0:DIFF_V_DIM, :] * (1.0 / a1[DIFF_V_DIM:DIFF_V_DIM + 1, :])))
        y = o * lax.rsqrt(jnp.mean(o * o, axis=0, keepdims=True) + NORM_EPS) * g_ref[...] * (1.0 - lam_init)
        o_ref[:, hd * LANES:(hd + 1) * LANES] = y.T.astype(o_ref.dtype)


def _diff_attn(qk, vt, lq1, lk1, lq2, lk2, g, batch, seq, lam_init):
    t = qk.shape[0]
    tq = TKV
    nq = seq // tq
    bias = _diff_bias_tiles()
    vec = lambda a: pl.BlockSpec(a.shape, lambda b, i: (0, 0))
    args = [a.reshape(1, -1) for a in (lq1, lk1, lq2, lk2)] + [g.reshape(-1, 1)]
    return pl.pallas_call(
        partial(_diff_attn_kernel, tq=tq, lam_init=lam_init),
        grid=(batch, nq),
        in_specs=[pl.BlockSpec((tq, DIFF_WIDTH), lambda b, i: (b * nq + i, 0)),
                  pl.BlockSpec((seq, DIFF_WIDTH), lambda b, i: (b, 1)),
                  pl.BlockSpec((nq,) + vt.shape[1:], lambda b, i: (b, 0, 0)),
                  pl.BlockSpec(bias.shape, lambda b, i: (0, 0, 0, 0))] + [vec(a) for a in args],
        out_specs=pl.BlockSpec((tq, DIFF_WIDTH), lambda b, i: (b * nq + i, 0)),
        out_shape=jax.ShapeDtypeStruct((t, DIFF_WIDTH), BF16),
        scratch_shapes=[pltpu.VMEM((2 * DIFF_HEADS, tq, LANES), BF16),
                        pltpu.VMEM((2 * DIFF_HEADS, tq), F32),
                        pltpu.VMEM((2 * DIFF_HEADS, DIFF_VT_ROWS, tq), F32)],
        compiler_params=_cp(2),
        name="diff_attn",
    )(qk, qk, vt, bias, *args)


def _chunk_attn_kernel(q_ref, k_ref, vt_ref, bias_ref, o_ref, qm_ref, stage_ref, *, tq):
    qi = pl.program_id(1)
    lo, hi = _half_masks(tq)
    scale = CHK_HEAD_DIM ** -0.5 * LOG2E
    for pair in range(CHK_HEADS // 2):
        q0, q1 = _masked_halves(q_ref[:, pair * LANES:(pair + 1) * LANES], scale, lo, hi)
        qm_ref[2 * pair] = q0
        qm_ref[2 * pair + 1] = q1

    def window(n_tiles):
        slots = range(CHK_WIN_TILES - n_tiles, CHK_WIN_TILES)

        def scores(hd):
            cols = slice((hd // 2) * LANES, (hd // 2 + 1) * LANES)
            out = []
            for w in slots:
                ks = pl.multiple_of((qi - (CHK_WIN_TILES - 1) + w) * tq, tq)
                out.append(_nt_dot(k_ref[pl.ds(ks, tq), cols], qm_ref[hd]) + bias_ref[hd, w])
            return out

        def softmax(hd, s_tiles):
            m = jnp.max(s_tiles[0], axis=0, keepdims=True)
            for s in s_tiles[1:]:
                m = jnp.maximum(m, jnp.max(s, axis=0, keepdims=True))
            return None, [jnp.exp2(s - m).astype(BF16) for s in s_tiles]

        def accumulate(hd, _, p_tiles):
            rows = slice(hd * CHK_VT_ROWS, (hd + 1) * CHK_VT_ROWS)
            acc = None
            for w, p in zip(slots, p_tiles):
                pv = jnp.dot(vt_ref[qi - (CHK_WIN_TILES - 1) + w, rows, :], p, preferred_element_type=F32)
                acc = pv if acc is None else acc + pv
            stage_ref[hd] = acc[0:CHK_HEAD_DIM, :] * (1.0 / acc[CHK_HEAD_DIM:CHK_HEAD_DIM + 1, :])

        _run_pipelined(list(range(CHK_HEADS)), scores, softmax, accumulate)

    for n_tiles in range(1, CHK_WIN_TILES):
        pl.when(qi == n_tiles - 1)(partial(window, n_tiles))
    pl.when(qi >= CHK_WIN_TILES - 1)(partial(window, CHK_WIN_TILES))

    for pair in range(CHK_HEADS // 2):
        both = jnp.concatenate([stage_ref[2 * pair], stage_ref[2 * pair + 1]], axis=0)
        o_ref[:, pair * LANES:(pair + 1) * LANES] = both.T.astype(o_ref.dtype)


def _chunk_attn(qk, vt, bias, batch, seq):
    t = qk.shape[0]
    tq = TKV
    nq = seq // tq
    return pl.pallas_call(
        partial(_chunk_attn_kernel, tq=tq),
        grid=(batch, nq),
        in_specs=[pl.BlockSpec((tq, CHK_WIDTH), lambda b, i: (b * nq + i, 0)),
                  pl.BlockSpec((seq, CHK_WIDTH), lambda b, i: (b, 1)),
                  pl.BlockSpec((nq,) + vt.shape[1:], lambda b, i: (b, 0, 0)),
                  pl.BlockSpec(bias.shape, lambda b, i: (0, 0, 0, 0))],
        out_specs=pl.BlockSpec((tq, CHK_WIDTH), lambda b, i: (b * nq + i, 0)),
        out_shape=jax.ShapeDtypeStruct((t, CHK_WIDTH), BF16),
        scratch_shapes=[pltpu.VMEM((CHK_HEADS, tq, LANES), BF16),
                        pltpu.VMEM((CHK_HEADS, CHK_HEAD_DIM, tq), F32)],
        compiler_params=_cp(2),
        name="chunk_attn",
    )(qk, qk, vt, bias)


def _toeplitz(seq, rows, cols):
    h, n = seq.shape
    period = rows + cols
    flat = jnp.tile(jnp.pad(seq, ((0, 0), (0, period - n))), (1, rows + 1))[:, :rows * (period + 1)]
    return flat.reshape(h, rows, period + 1)[:, :, :cols]


def _chunk_bias(rel_table):
    n_heads = rel_table.shape[0]
    kj = np.arange(TKV)[:, None]
    qi = np.arange(TKV)[None, :]
    ext = jnp.concatenate([rel_table[:, REL_CLIP - (TKV - 1):],
                           jnp.broadcast_to(rel_table[:, 2 * REL_CLIP:], (n_heads, 3 * TKV - 1 - REL_CLIP))],
                          axis=1) * LOG2E
    tiles = []
    for w in range(CHK_WIN_TILES):
        dist = (CHK_WIN_TILES - 1 - w) * TKV
        t = _toeplitz(ext[:, dist:dist + 2 * TKV - 1], TKV, TKV)[:, ::-1, :]
        dchunk = (qi + dist) // CHUNK - kj // CHUNK
        band = (dchunk >= 0) & (dchunk <= CHK_LEFT_CHUNKS)
        tiles.append(jnp.where(jnp.asarray(band)[None], t, NEG))
    return jnp.stack(tiles, axis=1).astype(F32)


def _mla_attn_kernel(q_ref, k_ref, vt_ref, o_ref, m_ref, acc_ref, *, tq):
    qi = pl.program_id(1)
    m_ref[...] = jnp.full(m_ref.shape, NEG, F32)
    acc_ref[...] = jnp.zeros(acc_ref.shape, F32)
    ki = lax.broadcasted_iota(jnp.int32, (tq, tq), 0)
    qj = lax.broadcasted_iota(jnp.int32, (tq, tq), 1)
    diag_ok = (ki // CHUNK) <= (qj // CHUNK)

    def scores(unit):
        kt, hd, masked = unit
        cols = slice(hd * LANES, (hd + 1) * LANES)
        s = _nt_dot(k_ref[pl.ds(pl.multiple_of(kt * tq, tq), tq), cols], q_ref[:, cols])
        return jnp.where(diag_ok, s, NEG) if masked else s

    def softmax_pv(unit, s):
        kt, hd, _ = unit
        m_prev = m_ref[hd:hd + 1, :]
        m_new = jnp.maximum(m_prev, jnp.max(s, axis=0, keepdims=True))
        m_ref[hd:hd + 1, :] = m_new
        p = jnp.exp2(s - m_new).astype(BF16)
        return jnp.exp2(m_prev - m_new), p

    def accumulate(unit, alpha, p):
        kt, hd, _ = unit
        pv = jnp.dot(vt_ref[kt, hd * MLA_VT_ROWS:(hd + 1) * MLA_VT_ROWS, :], p,
                     preferred_element_type=F32)
        acc_ref[hd] = alpha * acc_ref[hd] + pv

    def tiles(kts):
        _run_pipelined([(kt, hd, masked) for kt, masked in kts for hd in range(MLA_HEADS)],
                       scores, softmax_pv, accumulate)

    def body(j, carry):
        tiles([(2 * j, False), (2 * j + 1, False)])
        return carry

    lax.fori_loop(0, qi // 2, body, 0)

    @pl.when(qi % 2 == 1)
    def _():
        tiles([(qi - 1, False), (qi, True)])

    @pl.when(qi % 2 == 0)
    def _():
        tiles([(qi, True)])

    for pair in range(MLA_HEADS // 2):
        halves = []
        for hd in (2 * pair, 2 * pair + 1):
            acc = acc_ref[hd]
            halves.append(acc[0:MLA_V, :] * (1.0 / acc[MLA_V:MLA_V + 1, :]))
        o_ref[:, pair * LANES:(pair + 1) * LANES] = jnp.concatenate(halves, axis=0).T.astype(o_ref.dtype)


def _mla_attn(q, k, vt, batch, seq):
    t = q.shape[0]
    tq = TKV
    nq = seq // tq
    return pl.pallas_call(
        partial(_mla_attn_kernel, tq=tq),
        grid=(batch, nq),
        in_specs=[pl.BlockSpec((tq, q.shape[1]), lambda b, i: (b * nq + i, 0)),
                  pl.BlockSpec((seq, k.shape[1]), lambda b, i: (b, 0)),
                  pl.BlockSpec((nq,) + vt.shape[1:], lambda b, i: (b, 0, 0))],
        out_specs=pl.BlockSpec((tq, MLA_HEADS * MLA_V), lambda b, i: (b * nq + i, 0)),
        out_shape=jax.ShapeDtypeStruct((t, MLA_HEADS * MLA_V), BF16),
        scratch_shapes=[pltpu.VMEM((MLA_HEADS, tq), F32),
                        pltpu.VMEM((MLA_HEADS, MLA_VT_ROWS, tq), F32)],
        compiler_params=_cp(2),
        name="mla_attn",
    )(q, k, vt)


def _route(logits_t, bias_ref):
    e_rows = [logits_t[e:e + 1, :] for e in range(N_EXPERTS)]
    scores = [jax.nn.sigmoid(r) for r in e_rows]
    sel = [scores[e] + bias_ref[e] for e in range(N_EXPERTS)]
    gscore = []
    for g in range(N_GROUPS):
        s4 = sel[g * EXPERTS_PER_GROUP:(g + 1) * EXPERTS_PER_GROUP]
        best = None
        for a in range(EXPERTS_PER_GROUP):
            for b in range(a + 1, EXPERTS_PER_GROUP):
                pair = s4[a] + s4[b]
                best = pair if best is None else jnp.maximum(best, pair)
        gscore.append(best)
    gbest = gscore[0]
    gidx = jnp.zeros_like(gbest, dtype=jnp.int32)
    for g in range(1, N_GROUPS):
        better = gscore[g] > gbest
        gbest = jnp.where(better, gscore[g], gbest)
        gidx = jnp.where(better, g, gidx)
    masked = [jnp.where(gidx == e // EXPERTS_PER_GROUP, sel[e], -jnp.inf) for e in range(N_EXPERTS)]

    def arg_first_max(vals):
        top = vals[0]
        for v in vals[1:]:
            top = jnp.maximum(top, v)
        idx = jnp.full(top.shape, N_EXPERTS, jnp.int32)
        for e in range(N_EXPERTS - 1, -1, -1):
            idx = jnp.where(vals[e] == top, e, idx)
        return idx

    i0 = arg_first_max(masked)
    i1 = arg_first_max([jnp.where(i0 == e, -jnp.inf, masked[e]) for e in range(N_EXPERTS)])
    w0 = sum(jnp.where(i0 == e, scores[e], 0.0) for e in range(N_EXPERTS))
    w1 = sum(jnp.where(i1 == e, scores[e], 0.0) for e in range(N_EXPERTS))
    den = w0 + w1
    w0 = w0 / den
    w1 = w1 / den
    rows = [jnp.where(i0 == e, w0, 0.0) + jnp.where(i1 == e, w1, 0.0) for e in range(N_EXPERTS)]
    return jnp.concatenate(rows, axis=0)


def _merge_kernel(ya_ref, yb_ref, yc_ref, gate_ref, x_ref, gt_ref, sc_ref, sh_ref, g_ref,
                  wa_ref, wb_ref, wc_ref, wo_ref, wrh_ref, wrl_ref, rb_ref,
                  xo_ref, h_ref, comb_ref):
    d = D_MODEL
    tm = x_ref.shape[0]
    halves = [slice(0, tm // 2), slice(tm // 2, tm)]
    dot = partial(jnp.dot, preferred_element_type=F32)
    branches = [(dot(ya_ref[r, :], wa_ref[...]), dot(yb_ref[r, :], wb_ref[...]), dot(yc_ref[r, :], wc_ref[...]))
                for r in halves]
    merged = [(gate_ref[r, 0:d].astype(F32) * a + gate_ref[r, d:2 * d].astype(F32) * b
               + gate_ref[r, 2 * d:3 * d].astype(F32) * c).astype(BF16)
              for r, (a, b, c) in zip(halves, branches)]
    ys = [dot(m, wo_ref[...]) for m in merged]
    h2s = []
    for r, y in zip(halves, ys):
        x_new = x_ref[r, :] + gt_ref[0] * y
        xo_ref[r, :] = x_new
        h2 = _rms(x_new, g_ref[...]) * (1.0 + sc_ref[0]) + sh_ref[0]
        h_ref[r, :] = h2.astype(h_ref.dtype)
        h2s.append(h2)
    logits = []
    for h2 in h2s:
        hi = h2.astype(BF16)
        lo = (h2 - hi.astype(F32)).astype(BF16)
        logits.append(dot(hi, wrh_ref[...]) + (dot(lo, wrh_ref[...]) + dot(hi, wrl_ref[...])))
    pad_rows = jnp.zeros((LANES - N_EXPERTS, tm // 2), F32)
    for r, lg in zip(halves, logits):
        comb_t = _route(lg.T[0:N_EXPERTS, :], rb_ref)
        comb_ref[r, :] = jnp.concatenate([comb_t, pad_rows], axis=0).T


def _merge(ya, yb, yc, gates, x2, gt, sc, sh, g, w, wr_hi, wr_lo, rbias, seq):
    t, d = x2.shape
    tm = TM
    tpb = seq // tm
    row = lambda n: pl.BlockSpec((tm, n), lambda i: (i, 0))
    full = lambda a: pl.BlockSpec(a.shape, lambda i: (0,) * a.ndim)
    bspec = pl.BlockSpec((1, 1, d), lambda i: (i // tpb, 0, 0))
    ws = [w["br_diff"], w["br_chunk"], w["br_mla"], w["out"], wr_hi, wr_lo]
    g2 = g.reshape(1, d)
    return pl.pallas_call(
        _merge_kernel,
        grid=(t // tm,),
        in_specs=[row(ya.shape[1]), row(yb.shape[1]), row(yc.shape[1]), row(gates.shape[1]), row(d),
                  bspec, bspec, bspec, full(g2)] + [full(a) for a in ws]
                 + [pl.BlockSpec(memory_space=pltpu.SMEM)],
        out_specs=[row(d), row(d), row(LANES)],
        out_shape=[jax.ShapeDtypeStruct((t, d), F32), jax.ShapeDtypeStruct((t, d), BF16),
                   jax.ShapeDtypeStruct((t, LANES), F32)],
        compiler_params=_cp(1),
        name="merge_route",
    )(ya, yb, yc, gates, x2, gt, sc, sh, g2, *ws, rbias)


def _moe_kernel(h_ref, wg_ref, wu_ref, wd_ref, comb_ref, x_ref, gt_ref, o_ref, acc_ref):
    e = pl.program_id(1)

    @pl.when(e == 0)
    def _():
        acc_ref[...] = jnp.zeros_like(acc_ref)

    h = h_ref[...]
    a = jnp.dot(h, wg_ref[0].astype(BF16), preferred_element_type=F32)
    b = jnp.dot(h, wu_ref[0].astype(BF16), preferred_element_type=F32)
    he = (a * jax.nn.sigmoid(a) * b).astype(BF16)
    lane = lax.broadcasted_iota(jnp.int32, comb_ref.shape, 1)
    comb_e = jnp.sum(jnp.where(lane == e, comb_ref[...], 0.0), axis=1, keepdims=True)
    acc_ref[...] += comb_e * jnp.dot(he, wd_ref[0].astype(BF16), preferred_element_type=F32)

    @pl.when(e == pl.num_programs(1) - 1)
    def _():
        o_ref[...] = x_ref[...] + gt_ref[0] * acc_ref[...]


def _moe_dense(h2, comb, x2, gt, wg, wu, wd, layer, seq):
    t, d = x2.shape
    tm = TM_MOE
    tpb = seq // tm
    _, n_e, _, dff = wg.shape
    return pl.pallas_call(
        _moe_kernel,
        grid=(t // tm, n_e),
        in_specs=[pl.BlockSpec((tm, d), lambda i, e: (i, 0)),
                  pl.BlockSpec((None, 1, d, dff), lambda i, e: (layer, e, 0, 0)),
                  pl.BlockSpec((None, 1, d, dff), lambda i, e: (layer, e, 0, 0)),
                  pl.BlockSpec((None, 1, dff, d), lambda i, e: (layer, e, 0, 0)),
                  pl.BlockSpec((tm, LANES), lambda i, e: (i, 0)),
                  pl.BlockSpec((tm, d), lambda i, e: (i, 0)),
                  pl.BlockSpec((1, 1, d), lambda i, e: (i // tpb, 0, 0))],
        out_specs=pl.BlockSpec((tm, d), lambda i, e: (i, 0)),
        out_shape=jax.ShapeDtypeStruct((t, d), F32),
        scratch_shapes=[pltpu.VMEM((tm, d), F32)],
        compiler_params=_cp(2),
        name="moe_dense",
    )(h2, wg, wu, wd, comb, x2, gt)


def _final_norm_kernel(x_ref, g_ref, o_ref):
    o_ref[...] = _rms(x_ref[...], g_ref[...])


def _final_norm(x2, g):
    t, d = x2.shape
    return pl.pallas_call(
        _final_norm_kernel,
        grid=(t // TM,),
        in_specs=[pl.BlockSpec((TM, d), lambda i: (i, 0)), pl.BlockSpec((1, d), lambda i: (0, 0))],
        out_specs=pl.BlockSpec((TM, d), lambda i: (i, 0)),
        out_shape=jax.ShapeDtypeStruct((t, d), F32),
        compiler_params=_cp(1),
        name="final_norm",
    )(x2, g.reshape(1, d))


def _pad_heads(w, n_heads, width, start, stop, at=0):
    rows = w.shape[0]
    wh = w.reshape(rows, n_heads, width)[:, :, start:stop]
    out = jnp.zeros((rows, n_heads, LANES), w.dtype)
    out = out.at[:, :, at:at + (stop - start)].set(wh)
    return out.reshape(rows, n_heads * LANES)


def _rope_swap(w_rope):
    half = MLA_ROPE // 2
    return jnp.concatenate([-w_rope[..., half:], w_rope[..., :half]], axis=-1)


def _layer_weights(l, w_in, mla_q_norm_g, mla_w_q_b, mla_kv_norm_g, mla_w_kv_b,
                   w_branch_diff, w_branch_chunk, w_branch_mla, w_out):
    wi = w_in[l]
    o_chunk = 3 * DIFF_WIDTH
    o_mq = o_chunk + 3 * CHK_WIDTH
    o_mkv = o_mq + MLA_Q_LORA
    o_gate = o_mkv + MLA_KV_LORA + MLA_ROPE
    qk = MLA_NOPE + MLA_ROPE
    wqb = mla_w_q_b[l]
    wq_rope = wqb.reshape(MLA_Q_LORA, MLA_HEADS, qk)[:, :, MLA_NOPE:]
    wqs = jnp.zeros((MLA_Q_LORA, MLA_HEADS, LANES), F32).at[:, :, MLA_NOPE:qk].set(_rope_swap(wq_rope))
    wkr = wi[:, o_mkv + MLA_KV_LORA:o_gate]
    pad_kr = lambda a: jnp.zeros((D_MODEL, LANES), F32).at[:, MLA_NOPE:qk].set(a)
    wkvb = mla_w_kv_b[l]
    bf = lambda a: a.astype(BF16)

    def vt_weight(wv, n_heads, v_dim):
        padded = jnp.pad(wv.reshape(D_MODEL, n_heads, v_dim), ((0, 0), (0, 0), (0, ONES_ROWS)))
        return bf(padded.reshape(D_MODEL, n_heads * (v_dim + ONES_ROWS)).T)

    return {
        "diff_qk": bf(wi[:, :2 * DIFF_WIDTH]),
        "diff_vt": vt_weight(wi[:, 2 * DIFF_WIDTH:o_chunk], DIFF_HEADS, DIFF_V_DIM),
        "chunk_qk": bf(wi[:, o_chunk:o_chunk + 2 * CHK_WIDTH]),
        "chunk_vt": vt_weight(wi[:, o_chunk + 2 * CHK_WIDTH:o_mq], CHK_HEADS, CHK_HEAD_DIM),
        "gate": bf(wi[:, o_gate:]),
        "mq": bf(wi[:, o_mq:o_mkv]),
        "ckv": bf(wi[:, o_mkv:o_mkv + MLA_KV_LORA]),
        "kr": bf(pad_kr(wkr)),
        "krs": bf(pad_kr(_rope_swap(wkr))),
        "gq": mla_q_norm_g[l].reshape(1, -1),
        "gkv": mla_kv_norm_g[l].reshape(1, -1),
        "q": bf(_pad_heads(wqb, MLA_HEADS, qk, 0, qk)),
        "qs": bf(wqs.reshape(MLA_Q_LORA, MLA_HEADS * LANES)),
        "k": bf(_pad_heads(wkvb, MLA_HEADS, MLA_NOPE + MLA_V, 0, MLA_NOPE)),
        "vt": bf(jnp.pad(wkvb.reshape(MLA_KV_LORA, MLA_HEADS, MLA_NOPE + MLA_V)[:, :, MLA_NOPE:],
                         ((0, 0), (0, 0), (0, ONES_ROWS))).reshape(MLA_KV_LORA, MLA_HEADS * MLA_VT_ROWS).T),
        "br_diff": bf(w_branch_diff[l]),
        "br_chunk": bf(w_branch_chunk[l]),
        "br_mla": bf(w_branch_mla[l]),
        "out": bf(w_out[l]),
    }


def _rope_tables(seq):
    pos = jnp.arange(seq, dtype=F32)
    inv_freq = 1.0 / (ROPE_THETA ** (jnp.arange(0, MLA_ROPE, 2, dtype=F32) / MLA_ROPE))
    ang = pos[:, None] * inv_freq[None, :]
    cos, sin = jnp.cos(ang), jnp.sin(ang)
    qk = MLA_NOPE + MLA_ROPE
    cos_tab = jnp.zeros((seq, LANES), F32).at[:, :MLA_NOPE].set(1.0)
    cos_tab = cos_tab.at[:, MLA_NOPE:qk].set(jnp.concatenate([cos, cos], axis=-1))
    sin_tab = jnp.zeros((seq, LANES), F32).at[:, MLA_NOPE:qk].set(jnp.concatenate([sin, sin], axis=-1))
    return cos_tab, sin_tab


def kernel(x, c, w_mod, b_mod, g_norm_mix, g_norm_ffn, w_in, diff_lambda_q1, diff_lambda_k1, diff_lambda_q2, diff_lambda_k2, diff_subln_g, chunk_rel_bias, mla_q_norm_g, mla_w_q_b, mla_kv_norm_g, mla_w_kv_b, w_branch_diff, w_branch_chunk, w_branch_mla, w_out, w_router, router_bias, w_exp_gate, w_exp_up, w_exp_down, g_final):
    batch, seq, d = x.shape
    depth = w_in.shape[0]
    t = batch * seq
    x2 = x.reshape(t, d)
    cos_tab, sin_tab = _rope_tables(seq)
    mod = _modulation(c, w_mod, b_mod)
    wr_pad = jnp.pad(w_router, ((0, 0), (0, LANES - N_EXPERTS)))
    wr_hi = wr_pad.astype(BF16)
    wr_lo = (wr_pad - wr_hi.astype(F32)).astype(BF16)
    for l in range(depth):
        sh1, sc1, gt1, sh2, sc2, gt2 = [m.reshape(batch, 1, d) for m in jnp.split(mod[l], 6, axis=-1)]
        w = _layer_weights(l, w_in, mla_q_norm_g, mla_w_q_b, mla_kv_norm_g, mla_w_kv_b,
                           w_branch_diff, w_branch_chunk, w_branch_mla, w_out)
        h = _norm_mod(x2, g_norm_mix[l], sc1, sh1, seq)
        lam_init = 0.8 - 0.6 * math.exp(-0.3 * l)
        dqk, dvt = _qk_vt(h, w["diff_qk"], w["diff_vt"], DIFF_V_DIM, "diff_prep")
        ya = _diff_attn(dqk, dvt, diff_lambda_q1[l], diff_lambda_k1[l], diff_lambda_q2[l], diff_lambda_k2[l],
                        diff_subln_g[l], batch, seq, lam_init)
        cqk, cvt = _qk_vt(h, w["chunk_qk"], w["chunk_vt"], CHK_HEAD_DIM, "chunk_prep")
        yb = _chunk_attn(cqk, cvt, _chunk_bias(chunk_rel_bias[l]), batch, seq)
        q, k, v = _mla_prep(h, w, cos_tab, sin_tab, seq)
        yc = _mla_attn(q, k, v, batch, seq)
        gates = _proj(h, w["gate"], sigmoid=True, name="proj_gate")
        x2, h2, comb = _merge(ya, yb, yc, gates, x2, gt1, sc2, sh2, g_norm_ffn[l], w, wr_hi, wr_lo,
                                router_bias, seq)
        x2 = _moe_dense(h2, comb, x2, gt2, w_exp_gate, w_exp_up, w_exp_down, l, seq)
    return _final_norm(x2, g_final).reshape(batch, seq, d)
```

```python
import math
from functools import partial

import jax
import jax.numpy as jnp
import numpy as np
from jax import lax
from jax.experimental import pallas as pl
from jax.experimental.pallas import tpu as pltpu

F32 = jnp.float32
BF16 = jnp.bfloat16

D_MODEL = 1024
CHUNK = 64
NORM_EPS = 1e-6
DIFF_HEADS = 4
DIFF_HEAD_DIM = 64
DIFF_V_DIM = 128
DIFF_WIDTH = 512
CHK_HEADS = 8
CHK_HEAD_DIM = 64
CHK_WIDTH = 512
CHK_LEFT_CHUNKS = 8
REL_CLIP = 256
MLA_HEADS = 8
MLA_Q_LORA = 384
MLA_KV_LORA = 256
MLA_NOPE = 64
MLA_ROPE = 32
MLA_V = 64
ROPE_THETA = 10000.0
N_EXPERTS = 16
N_GROUPS = 4
EXPERTS_PER_GROUP = 4
MOE_D_FF = 512

LANES = 128
NEG = -1e30
VMEM_LIMIT = 56 * 1024 * 1024

TM = 512
TKV = 256
ONES_ROWS = 16
MLA_VT_ROWS = MLA_V + ONES_ROWS
DIFF_VT_ROWS = DIFF_V_DIM + ONES_ROWS
LOG2E = math.log2(math.e)
CHK_VT_ROWS = CHK_HEAD_DIM + ONES_ROWS
CHK_WIN_TILES = CHK_LEFT_CHUNKS * CHUNK // TKV + 1
TM_MOE = 1024


def _cp(n_axes):
    return pltpu.CompilerParams(dimension_semantics=("arbitrary",) * n_axes,
                                vmem_limit_bytes=VMEM_LIMIT)


def _rms(x, g):
    return x * lax.rsqrt(jnp.mean(x * x, axis=-1, keepdims=True) + NORM_EPS) * g


def _mod_kernel(c_ref, w_ref, b_ref, o_ref):
    c = c_ref[...]
    c_act = c * jax.nn.sigmoid(c)
    o_ref[0] = jnp.dot(c_act.astype(BF16), w_ref[0].astype(BF16),
                       preferred_element_type=F32) + b_ref[0]


def _modulation(c, w_mod, b_mod):
    n_layers, d, n = w_mod.shape
    b = c.shape[0]
    tn = 1024
    return pl.pallas_call(
        _mod_kernel,
        grid=(n_layers, n // tn),
        in_specs=[pl.BlockSpec((b, d), lambda l, j: (0, 0)),
                  pl.BlockSpec((1, d, tn), lambda l, j: (l, 0, j)),
                  pl.BlockSpec((1, 1, tn), lambda l, j: (l, 0, j))],
        out_specs=pl.BlockSpec((1, b, tn), lambda l, j: (l, 0, j)),
        out_shape=jax.ShapeDtypeStruct((n_layers, b, n), F32),
        compiler_params=_cp(2),
        name="adaln_mod",
    )(c, w_mod, b_mod.reshape(n_layers, 1, n))


def _norm_mod_kernel(x_ref, g_ref, sc_ref, sh_ref, o_ref):
    h = _rms(x_ref[...], g_ref[...]) * (1.0 + sc_ref[0]) + sh_ref[0]
    o_ref[...] = h.astype(o_ref.dtype)


def _norm_mod(x2, g, sc, sh, seq):
    t, d = x2.shape
    tpb = seq // TM
    bspec = pl.BlockSpec((1, 1, d), lambda i: (i // tpb, 0, 0))
    return pl.pallas_call(
        _norm_mod_kernel,
        grid=(t // TM,),
        in_specs=[pl.BlockSpec((TM, d), lambda i: (i, 0)),
                  pl.BlockSpec((1, d), lambda i: (0, 0)), bspec, bspec],
        out_specs=pl.BlockSpec((TM, d), lambda i: (i, 0)),
        out_shape=jax.ShapeDtypeStruct((t, d), BF16),
        compiler_params=_cp(1),
        name="norm_mod",
    )(x2, g.reshape(1, d), sc, sh)


def _proj_kernel(h_ref, w_ref, o_ref, *, tn, sigmoid):
    h = h_ref[...]
    for j in range(w_ref.shape[1] // tn):
        r = jnp.dot(h, w_ref[:, j * tn:(j + 1) * tn], preferred_element_type=F32)
        if sigmoid:
            r = jax.nn.sigmoid(r)
        o_ref[:, j * tn:(j + 1) * tn] = r.astype(o_ref.dtype)


def _proj(h, w, sigmoid=False, name="proj"):
    t, d = h.shape
    n = w.shape[1]
    return pl.pallas_call(
        partial(_proj_kernel, tn=512, sigmoid=sigmoid),
        grid=(t // TM,),
        in_specs=[pl.BlockSpec((TM, d), lambda i: (i, 0)),
                  pl.BlockSpec((d, n), lambda i: (0, 0))],
        out_specs=pl.BlockSpec((TM, n), lambda i: (i, 0)),
        out_shape=jax.ShapeDtypeStruct((t, n), BF16),
        compiler_params=_cp(1),
        name=name,
    )(h, w)


def _mla_prep_kernel(h_ref, wmq_ref, wckv_ref, wkr_ref, wkrs_ref, gq_ref, gkv_ref,
                     wq_ref, wqs_ref, wk_ref, wvt_ref, cos_ref, sin_ref,
                     q_ref, k_ref, vt_ref, *, scale):
    dot = partial(jnp.dot, preferred_element_type=F32)
    parts = [slice(j * TKV, (j + 1) * TKV) for j in range(vt_ref.shape[0])]
    low = [(dot(h_ref[r, :], wmq_ref[...]), dot(h_ref[r, :], wckv_ref[...]),
            dot(h_ref[r, :], wkr_ref[...]), dot(h_ref[r, :], wkrs_ref[...])) for r in parts]
    qn = [_rms(mq, gq_ref[...]).astype(BF16) for mq, _, _, _ in low]
    cn = [_rms(ckv, gkv_ref[...]).astype(BF16) for _, ckv, _, _ in low]
    kr = [a * cos_ref[r, :] + b * sin_ref[r, :] for r, (_, _, a, b) in zip(parts, low)]
    wide = [(dot(q, wq_ref[...]), dot(q, wqs_ref[...]), dot(c, wk_ref[...])) for q, c in zip(qn, cn)]
    for r, (qa, qb, ka), kr_r in zip(parts, wide, kr):
        cos = cos_ref[r, :]
        sin = sin_ref[r, :]
        for hd in range(MLA_HEADS):
            cols = slice(hd * LANES, (hd + 1) * LANES)
            q_ref[r, cols] = ((qa[:, cols] * cos + qb[:, cols] * sin) * scale).astype(q_ref.dtype)
            k_ref[r, cols] = (ka[:, cols] + kr_r).astype(k_ref.dtype)
    row = lax.broadcasted_iota(jnp.int32, (MLA_HEADS * MLA_VT_ROWS, 1), 0)
    ones_col = jnp.where(row % MLA_VT_ROWS >= MLA_V, 1.0, 0.0)
    for j, c in enumerate(cn):
        vt_ref[j] = (_nt_dot(wvt_ref[...], c) + ones_col).astype(vt_ref.dtype)


def _mla_prep(h, w, cos_tab, sin_tab, seq):
    t, d = h.shape
    tpb = seq // TM
    full = lambda a: pl.BlockSpec(a.shape, lambda i: (0,) * a.ndim)
    tab = pl.BlockSpec((TM, LANES), lambda i: (i % tpb, 0))
    ws = [w["mq"], w["ckv"], w["kr"], w["krs"], w["gq"], w["gkv"], w["q"], w["qs"], w["k"], w["vt"]]
    scale = (MLA_NOPE + MLA_ROPE) ** -0.5 * LOG2E
    vt_rows = MLA_HEADS * MLA_VT_ROWS
    return pl.pallas_call(
        partial(_mla_prep_kernel, scale=scale),
        grid=(t // TM,),
        in_specs=[pl.BlockSpec((TM, d), lambda i: (i, 0))] + [full(a) for a in ws] + [tab, tab],
        out_specs=[pl.BlockSpec((TM, MLA_HEADS * LANES), lambda i: (i, 0)),
                   pl.BlockSpec((TM, MLA_HEADS * LANES), lambda i: (i, 0)),
                   pl.BlockSpec((TM // TKV, vt_rows, TKV), lambda i: (i, 0, 0))],
        out_shape=[jax.ShapeDtypeStruct((t, MLA_HEADS * LANES), BF16),
                   jax.ShapeDtypeStruct((t, MLA_HEADS * LANES), BF16),
                   jax.ShapeDtypeStruct((t // TKV, vt_rows, TKV), BF16)],
        compiler_params=_cp(1),
        name="mla_prep",
    )(h, *ws, cos_tab, sin_tab)


def _nt_dot(a, b):
    return lax.dot_general(a, b, (((1,), (1,)), ((), ())), preferred_element_type=F32)


PIPE_DEPTH = 4


def _run_pipelined(units, scores, softmax, accumulate):
    n = len(units)
    pending = {i: scores(units[i]) for i in range(min(PIPE_DEPTH, n))}
    for i in range(n):
        alpha, p = softmax(units[i], pending.pop(i))
        if i + PIPE_DEPTH < n:
            pending[i + PIPE_DEPTH] = scores(units[i + PIPE_DEPTH])
        accumulate(units[i], alpha, p)


TILE_GROUP = 4


def _causal_tile_loop(qi, tiles):
    def body(j, carry):
        tiles([(TILE_GROUP * j + g, False) for g in range(TILE_GROUP)])
        return carry

    lax.fori_loop(0, qi // TILE_GROUP, body, 0)
    base = (qi // TILE_GROUP) * TILE_GROUP
    for rem in range(TILE_GROUP):
        pl.when(qi % TILE_GROUP == rem)(
            lambda rem=rem: tiles([(base + g, False) for g in range(rem)] + [(qi, True)]))


def _half_masks(rows):
    lane = lax.broadcasted_iota(jnp.int32, (rows, LANES), 1)
    lo = (lane < LANES // 2)
    return lo, jnp.logical_not(lo)


def _masked_halves(q, scale, lo, hi):
    qf = q.astype(F32) * scale
    return jnp.where(lo, qf, 0.0).astype(BF16), jnp.where(hi, qf, 0.0).astype(BF16)


def _qk_vt_kernel(h_ref, wqk_ref, wvt_ref, qk_ref, vt_ref, *, v_dim):
    h = h_ref[...]
    tn = 512
    for j in range(wqk_ref.shape[1] // tn):
        qk_ref[:, j * tn:(j + 1) * tn] = jnp.dot(h, wqk_ref[:, j * tn:(j + 1) * tn],
                                                 preferred_element_type=F32).astype(qk_ref.dtype)
    row = lax.broadcasted_iota(jnp.int32, (wvt_ref.shape[0], 1), 0)
    ones_col = jnp.where(row % (v_dim + ONES_ROWS) >= v_dim, 1.0, 0.0)
    vt = (_nt_dot(wvt_ref[...], h) + ones_col).astype(vt_ref.dtype)
    for j in range(vt_ref.shape[0]):
        vt_ref[j] = vt[:, j * TKV:(j + 1) * TKV]


def _qk_vt(h, wqk, wvt, v_dim, name):
    t, d = h.shape
    n = wqk.shape[1]
    vt_rows = wvt.shape[0]
    return pl.pallas_call(
        partial(_qk_vt_kernel, v_dim=v_dim),
        grid=(t // TM,),
        in_specs=[pl.BlockSpec((TM, d), lambda i: (i, 0)),
                  pl.BlockSpec((d, n), lambda i: (0, 0)),
                  pl.BlockSpec((vt_rows, d), lambda i: (0, 0))],
        out_specs=[pl.BlockSpec((TM, n), lambda i: (i, 0)),
                   pl.BlockSpec((TM // TKV, vt_rows, TKV), lambda i: (i, 0, 0))],
        out_shape=[jax.ShapeDtypeStruct((t, n), BF16),
                   jax.ShapeDtypeStruct((t // TKV, vt_rows, TKV), BF16)],
        compiler_params=_cp(1),
        name=name,
    )(h, wqk, wvt)


def _diff_bias_tiles():
    kj = np.arange(TKV)[:, None]
    qi = np.arange(TKV)[None, :]
    rel = (qi - kj).astype(np.float64)
    ok = (kj // CHUNK) <= (qi // CHUNK)
    off, diag = [], []
    for hd in range(DIFF_HEADS):
        c = 2.0 ** (-8.0 / DIFF_HEADS * (hd + 1)) * LOG2E
        off.append(-c * rel)
        diag.append(np.where(ok, -c * np.abs(rel), NEG))
    return np.stack([np.stack(off), np.stack(diag)]).astype(np.float32)


def _diff_attn_kernel(q_ref, k_ref, vt_ref, bias_ref, lq1_ref, lk1_ref, lq2_ref, lk2_ref, g_ref, o_ref,
                      qm_ref, m_ref, acc_ref, *, tq, lam_init):
    qi = pl.program_id(1)
    lo, hi = _half_masks(tq)
    scale = DIFF_HEAD_DIM ** -0.5 * LOG2E
    for hd in range(DIFF_HEADS):
        q0, q1 = _masked_halves(q_ref[:, hd * LANES:(hd + 1) * LANES], scale, lo, hi)
        qm_ref[2 * hd] = q0
        qm_ref[2 * hd + 1] = q1
    m_ref[...] = jnp.full(m_ref.shape, NEG, F32)
    acc_ref[...] = jnp.zeros(acc_ref.shape, F32)

    def scores(unit):
        kt, hm, diag = unit
        hd = hm // 2
        k = k_ref[pl.ds(pl.multiple_of(kt * tq, tq), tq), hd * LANES:(hd + 1) * LANES]
        return _nt_dot(k, qm_ref[hm]) + bias_ref[1 if diag else 0, hd]

    def softmax(unit, s):
        kt, hm, diag = unit
        c = 2.0 ** (-8.0 / DIFF_HEADS * (hm // 2 + 1)) * LOG2E
        shift = 0.0 if diag else (-c * tq) * (qi - kt).astype(F32)
        m_prev = m_ref[hm:hm + 1, :]
        m_new = jnp.maximum(m_prev, jnp.max(s, axis=0, keepdims=True) + shift)
        m_ref[hm:hm + 1, :] = m_new
        p = jnp.exp2(s - (m_new - shift)).astype(BF16)
        return jnp.exp2(m_prev - m_new), p

    def accumulate(unit, alpha, p):
        kt, hm, _ = unit
        hd = hm // 2
        pv = jnp.dot(vt_ref[kt, hd * DIFF_VT_ROWS:(hd + 1) * DIFF_VT_ROWS, :], p,
                     preferred_element_type=F32)
        acc_ref[hm] = alpha * acc_ref[hm] + pv

    def tiles(kts):
        _run_pipelined([(kt, hm, diag) for kt, diag in kts for hm in range(2 * DIFF_HEADS)],
                       scores, softmax, accumulate)

    _causal_tile_loop(qi, tiles)

    lam = (jnp.exp(jnp.sum(lq1_ref[...] * lk1_ref[...], axis=-1, keepdims=True))
           - jnp.exp(jnp.sum(lq2_ref[...] * lk2_ref[...], axis=-1, keepdims=True)) + lam_init)
    for hd in range(DIFF_HEADS):
        a0 = acc_ref[2 * hd]
        a1 = acc_ref[2 * hd + 1]
        o = (a0[0:DIFF_V_DIM, :] * (1.0 / a0[DIFF_V_DIM:DIFF_V_DIM + 1, :])
             - lam * (a1[0:DIFF_V_DIM, :] * (1.0 / a1[DIFF_V_DIM:DIFF_V_DIM + 1, :])))
        y = o * lax.rsqrt(jnp.mean(o * o, axis=0, keepdims=True) + NORM_EPS) * g_ref[...] * (1.0 - lam_init)
        o_ref[:, hd * LANES:(hd + 1) * LANES] = y.T.astype(o_ref.dtype)


def _diff_attn(qk, vt, lq1, lk1, lq2, lk2, g, batch, seq, lam_init):
    t = qk.shape[0]
    tq = TKV
    nq = seq // tq
    bias = _diff_bias_tiles()
    vec = lambda a: pl.BlockSpec(a.shape, lambda b, i: (0, 0))
    args = [a.reshape(1, -1) for a in (lq1, lk1, lq2, lk2)] + [g.reshape(-1, 1)]
    return pl.pallas_call(
        partial(_diff_attn_kernel, tq=tq, lam_init=lam_init),
        grid=(batch, nq),
        in_specs=[pl.BlockSpec((tq, DIFF_WIDTH), lambda b, i: (b * nq + i, 0)),
                  pl.BlockSpec((seq, DIFF_WIDTH), lambda b, i: (b, 1)),
                  pl.BlockSpec((nq,) + vt.shape[1:], lambda b, i: (b, 0, 0)),
                  pl.BlockSpec(bias.shape, lambda b, i: (0, 0, 0, 0))] + [vec(a) for a in args],
        out_specs=pl.BlockSpec((tq, DIFF_WIDTH), lambda b, i: (b * nq + i, 0)),
        out_shape=jax.ShapeDtypeStruct((t, DIFF_WIDTH), BF16),
        scratch_shapes=[pltpu.VMEM((2 * DIFF_HEADS, tq, LANES), BF16),
                        pltpu.VMEM((2 * DIFF_HEADS, tq), F32),
                        pltpu.VMEM((2 * DIFF_HEADS, DIFF_VT_ROWS, tq), F32)],
        compiler_params=_cp(2),
        name="diff_attn",
    )(qk, qk, vt, bias, *args)


def _chunk_attn_kernel(q_ref, k_ref, vt_ref, bias_ref, o_ref, qm_ref, stage_ref, *, tq):
    qi = pl.program_id(1)
    lo, hi = _half_masks(tq)
    scale = CHK_HEAD_DIM ** -0.5 * LOG2E
    for pair in range(CHK_HEADS // 2):
        q0, q1 = _masked_halves(q_ref[:, pair * LANES:(pair + 1) * LANES], scale, lo, hi)
        qm_ref[2 * pair] = q0
        qm_ref[2 * pair + 1] = q1

    def window(n_tiles):
        slots = range(CHK_WIN_TILES - n_tiles, CHK_WIN_TILES)

        def scores(hd):
            cols = slice((hd // 2) * LANES, (hd // 2 + 1) * LANES)
            out = []
            for w in slots:
                ks = pl.multiple_of((qi - (CHK_WIN_TILES - 1) + w) * tq, tq)
                out.append(_nt_dot(k_ref[pl.ds(ks, tq), cols], qm_ref[hd]) + bias_ref[hd, w])
            return out

        def softmax(hd, s_tiles):
            m = jnp.max(s_tiles[0], axis=0, keepdims=True)
            for s in s_tiles[1:]:
                m = jnp.maximum(m, jnp.max(s, axis=0, keepdims=True))
            return None, [jnp.exp2(s - m).astype(BF16) for s in s_tiles]

        def accumulate(hd, _, p_tiles):
            rows = slice(hd * CHK_VT_ROWS, (hd + 1) * CHK_VT_ROWS)
            acc = None
            for w, p in zip(slots, p_tiles):
                pv = jnp.dot(vt_ref[qi - (CHK_WIN_TILES - 1) + w, rows, :], p, preferred_element_type=F32)
                acc = pv if acc is None else acc + pv
            stage_ref[hd] = acc[0:CHK_HEAD_DIM, :] * (1.0 / acc[CHK_HEAD_DIM:CHK_HEAD_DIM + 1, :])

        _run_pipelined(list(range(CHK_HEADS)), scores, softmax, accumulate)

    for n_tiles in range(1, CHK_WIN_TILES):
        pl.when(qi == n_tiles - 1)(partial(window, n_tiles))
    pl.when(qi >= CHK_WIN_TILES - 1)(partial(window, CHK_WIN_TILES))

    for pair in range(CHK_HEADS // 2):
        both = jnp.concatenate([stage_ref[2 * pair], stage_ref[2 * pair + 1]], axis=0)
        o_ref[:, pair * LANES:(pair + 1) * LANES] = both.T.astype(o_ref.dtype)


def _chunk_attn(qk, vt, bias, layer, batch, seq):
    t = qk.shape[0]
    tq = TKV
    nq = seq // tq
    return pl.pallas_call(
        partial(_chunk_attn_kernel, tq=tq),
        grid=(batch, nq),
        in_specs=[pl.BlockSpec((tq, CHK_WIDTH), lambda b, i: (b * nq + i, 0)),
                  pl.BlockSpec((seq, CHK_WIDTH), lambda b, i: (b, 1)),
                  pl.BlockSpec((nq,) + vt.shape[1:], lambda b, i: (b, 0, 0)),
                  pl.BlockSpec((None,) + bias.shape[1:], lambda b, i: (layer, 0, 0, 0, 0))],
        out_specs=pl.BlockSpec((tq, CHK_WIDTH), lambda b, i: (b * nq + i, 0)),
        out_shape=jax.ShapeDtypeStruct((t, CHK_WIDTH), BF16),
        scratch_shapes=[pltpu.VMEM((CHK_HEADS, tq, LANES), BF16),
                        pltpu.VMEM((CHK_HEADS, CHK_HEAD_DIM, tq), F32)],
        compiler_params=_cp(2),
        name="chunk_attn",
    )(qk, qk, vt, bias)


def _chunk_bias(rel_tables):
    n_layers, n_heads, _ = rel_tables.shape
    period = 2 * TKV
    ext = jnp.concatenate([rel_tables[..., REL_CLIP - TKV:],
                           jnp.broadcast_to(rel_tables[..., 2 * REL_CLIP:], (n_layers, n_heads, 3 * TKV - 1 - REL_CLIP))],
                          axis=-1) * LOG2E
    seqs, bands = [], []
    kj = np.arange(TKV)[:, None]
    qi = np.arange(TKV)[None, :]
    for w in range(CHK_WIN_TILES):
        dist = (CHK_WIN_TILES - 1 - w) * TKV
        seqs.append(jnp.concatenate([ext[..., dist + TKV:dist + 2 * TKV], ext[..., dist:dist + TKV]], axis=-1))
        dchunk = (qi + dist) // CHUNK - kj // CHUNK
        bands.append((dchunk >= 0) & (dchunk <= CHK_LEFT_CHUNKS))
    seq = jnp.stack(seqs, axis=2).reshape(-1, period)
    flat = jnp.tile(seq, (1, TKV))[:, :TKV * (period - 1)]
    bias = flat.reshape(n_layers, n_heads, CHK_WIN_TILES, TKV, period - 1)[..., :TKV]
    return jnp.where(jnp.asarray(np.stack(bands)), bias, NEG).astype(F32)


def _mla_attn_kernel(q_ref, k_ref, vt_ref, o_ref, m_ref, acc_ref, *, tq):
    qi = pl.program_id(1)
    m_ref[...] = jnp.full(m_ref.shape, NEG, F32)
    acc_ref[...] = jnp.zeros(acc_ref.shape, F32)
    ki = lax.broadcasted_iota(jnp.int32, (tq, tq), 0)
    qj = lax.broadcasted_iota(jnp.int32, (tq, tq), 1)
    diag_ok = (ki // CHUNK) <= (qj // CHUNK)

    def scores(unit):
        kt, hd, masked = unit
        cols = slice(hd * LANES, (hd + 1) * LANES)
        s = _nt_dot(k_ref[pl.ds(pl.multiple_of(kt * tq, tq), tq), cols], q_ref[:, cols])
        return jnp.where(diag_ok, s, NEG) if masked else s

    def softmax_pv(unit, s):
        kt, hd, _ = unit
        m_prev = m_ref[hd:hd + 1, :]
        m_new = jnp.maximum(m_prev, jnp.max(s, axis=0, keepdims=True))
        m_ref[hd:hd + 1, :] = m_new
        p = jnp.exp2(s - m_new).astype(BF16)
        return jnp.exp2(m_prev - m_new), p

    def accumulate(unit, alpha, p):
        kt, hd, _ = unit
        pv = jnp.dot(vt_ref[kt, hd * MLA_VT_ROWS:(hd + 1) * MLA_VT_ROWS, :], p,
                     preferred_element_type=F32)
        acc_ref[hd] = alpha * acc_ref[hd] + pv

    def tiles(kts):
        _run_pipelined([(kt, hd, masked) for kt, masked in kts for hd in range(MLA_HEADS)],
                       scores, softmax_pv, accumulate)

    _causal_tile_loop(qi, tiles)

    for pair in range(MLA_HEADS // 2):
        halves = []
        for hd in (2 * pair, 2 * pair + 1):
            acc = acc_ref[hd]
            halves.append(acc[0:MLA_V, :] * (1.0 / acc[MLA_V:MLA_V + 1, :]))
        o_ref[:, pair * LANES:(pair + 1) * LANES] = jnp.concatenate(halves, axis=0).T.astype(o_ref.dtype)


def _mla_attn(q, k, vt, batch, seq):
    t = q.shape[0]
    tq = TKV
    nq = seq // tq
    return pl.pallas_call(
        partial(_mla_attn_kernel, tq=tq),
        grid=(batch, nq),
        in_specs=[pl.BlockSpec((tq, q.shape[1]), lambda b, i: (b * nq + i, 0)),
                  pl.BlockSpec((seq, k.shape[1]), lambda b, i: (b, 0)),
                  pl.BlockSpec((nq,) + vt.shape[1:], lambda b, i: (b, 0, 0))],
        out_specs=pl.BlockSpec((tq, MLA_HEADS * MLA_V), lambda b, i: (b * nq + i, 0)),
        out_shape=jax.ShapeDtypeStruct((t, MLA_HEADS * MLA_V), BF16),
        scratch_shapes=[pltpu.VMEM((MLA_HEADS, tq), F32),
                        pltpu.VMEM((MLA_HEADS, MLA_VT_ROWS, tq), F32)],
        compiler_params=_cp(2),
        name="mla_attn",
    )(q, k, vt)


def _route(logits_t, bias_ref):
    e_rows = [logits_t[e:e + 1, :] for e in range(N_EXPERTS)]
    scores = [jax.nn.sigmoid(r) for r in e_rows]
    sel = [scores[e] + bias_ref[e] for e in range(N_EXPERTS)]
    gscore = []
    for g in range(N_GROUPS):
        s4 = sel[g * EXPERTS_PER_GROUP:(g + 1) * EXPERTS_PER_GROUP]
        best = None
        for a in range(EXPERTS_PER_GROUP):
            for b in range(a + 1, EXPERTS_PER_GROUP):
                pair = s4[a] + s4[b]
                best = pair if best is None else jnp.maximum(best, pair)
        gscore.append(best)
    gbest = gscore[0]
    gidx = jnp.zeros_like(gbest, dtype=jnp.int32)
    for g in range(1, N_GROUPS):
        better = gscore[g] > gbest
        gbest = jnp.where(better, gscore[g], gbest)
        gidx = jnp.where(better, g, gidx)
    masked = [jnp.where(gidx == e // EXPERTS_PER_GROUP, sel[e], -jnp.inf) for e in range(N_EXPERTS)]

    def arg_first_max(vals):
        top = vals[0]
        for v in vals[1:]:
            top = jnp.maximum(top, v)
        idx = jnp.full(top.shape, N_EXPERTS, jnp.int32)
        for e in range(N_EXPERTS - 1, -1, -1):
            idx = jnp.where(vals[e] == top, e, idx)
        return idx

    i0 = arg_first_max(masked)
    i1 = arg_first_max([jnp.where(i0 == e, -jnp.inf, masked[e]) for e in range(N_EXPERTS)])
    w0 = sum(jnp.where(i0 == e, scores[e], 0.0) for e in range(N_EXPERTS))
    w1 = sum(jnp.where(i1 == e, scores[e], 0.0) for e in range(N_EXPERTS))
    den = w0 + w1
    w0 = w0 / den
    w1 = w1 / den
    rows = [jnp.where(i0 == e, w0, 0.0) + jnp.where(i1 == e, w1, 0.0) for e in range(N_EXPERTS)]
    return jnp.concatenate(rows, axis=0)


def _merge_kernel(ya_ref, yb_ref, yc_ref, gate_ref, x_ref, gt_ref, sc_ref, sh_ref, g_ref,
                  wa_ref, wb_ref, wc_ref, wo_ref, wrh_ref, wrl_ref, rb_ref,
                  xo_ref, h_ref, comb_ref):
    d = D_MODEL
    tm = x_ref.shape[0]
    halves = [slice(0, tm // 2), slice(tm // 2, tm)]
    dot = partial(jnp.dot, preferred_element_type=F32)
    branches = [(dot(ya_ref[r, :], wa_ref[...]), dot(yb_ref[r, :], wb_ref[...]), dot(yc_ref[r, :], wc_ref[...]))
                for r in halves]
    merged = [(gate_ref[r, 0:d].astype(F32) * a + gate_ref[r, d:2 * d].astype(F32) * b
               + gate_ref[r, 2 * d:3 * d].astype(F32) * c).astype(BF16)
              for r, (a, b, c) in zip(halves, branches)]
    ys = [dot(m, wo_ref[...]) for m in merged]
    h2s = []
    for r, y in zip(halves, ys):
        x_new = x_ref[r, :] + gt_ref[0] * y
        xo_ref[r, :] = x_new
        h2 = _rms(x_new, g_ref[...]) * (1.0 + sc_ref[0]) + sh_ref[0]
        h_ref[r, :] = h2.astype(h_ref.dtype)
        h2s.append(h2)
    logits = []
    for h2 in h2s:
        hi = h2.astype(BF16)
        lo = (h2 - hi.astype(F32)).astype(BF16)
        logits.append(dot(hi, wrh_ref[...]) + (dot(lo, wrh_ref[...]) + dot(hi, wrl_ref[...])))
    pad_rows = jnp.zeros((LANES - N_EXPERTS, tm // 2), F32)
    for r, lg in zip(halves, logits):
        comb_t = _route(lg.T[0:N_EXPERTS, :], rb_ref)
        comb_ref[r, :] = jnp.concatenate([comb_t, pad_rows], axis=0).T


def _merge(ya, yb, yc, gates, x2, gt, sc, sh, g, w, wr_hi, wr_lo, rbias, seq):
    t, d = x2.shape
    tm = TM
    tpb = seq // tm
    row = lambda n: pl.BlockSpec((tm, n), lambda i: (i, 0))
    full = lambda a: pl.BlockSpec(a.shape, lambda i: (0,) * a.ndim)
    bspec = pl.BlockSpec((1, 1, d), lambda i: (i // tpb, 0, 0))
    ws = [w["br_diff"], w["br_chunk"], w["br_mla"], w["out"], wr_hi, wr_lo]
    g2 = g.reshape(1, d)
    return pl.pallas_call(
        _merge_kernel,
        grid=(t // tm,),
        in_specs=[row(ya.shape[1]), row(yb.shape[1]), row(yc.shape[1]), row(gates.shape[1]), row(d),
                  bspec, bspec, bspec, full(g2)] + [full(a) for a in ws]
                 + [pl.BlockSpec(memory_space=pltpu.SMEM)],
        out_specs=[row(d), row(d), row(LANES)],
        out_shape=[jax.ShapeDtypeStruct((t, d), F32), jax.ShapeDtypeStruct((t, d), BF16),
                   jax.ShapeDtypeStruct((t, LANES), F32)],
        compiler_params=_cp(1),
        name="merge_route",
    )(ya, yb, yc, gates, x2, gt, sc, sh, g2, *ws, rbias)


def _moe_kernel(h_ref, wg_ref, wu_ref, wd_ref, comb_ref, x_ref, gt_ref, gf_ref, o_ref, acc_ref, *, final_norm):
    e = pl.program_id(1)

    @pl.when(e == 0)
    def _():
        acc_ref[...] = jnp.zeros_like(acc_ref)

    h = h_ref[...]
    a = jnp.dot(h, wg_ref[0].astype(BF16), preferred_element_type=F32)
    b = jnp.dot(h, wu_ref[0].astype(BF16), preferred_element_type=F32)
    he = (a * jax.nn.sigmoid(a) * b).astype(BF16)
    lane = lax.broadcasted_iota(jnp.int32, comb_ref.shape, 1)
    comb_e = jnp.sum(jnp.where(lane == e, comb_ref[...], 0.0), axis=1, keepdims=True)
    acc_ref[...] += comb_e * jnp.dot(he, wd_ref[0].astype(BF16), preferred_element_type=F32)

    @pl.when(e == pl.num_programs(1) - 1)
    def _():
        x_new = x_ref[...] + gt_ref[0] * acc_ref[...]
        o_ref[...] = _rms(x_new, gf_ref[...]) if final_norm else x_new


def _moe_dense(h2, comb, x2, gt, wg, wu, wd, layer, g_final, final_norm, seq):
    t, d = x2.shape
    tm = TM_MOE
    tpb = seq // tm
    _, n_e, _, dff = wg.shape
    return pl.pallas_call(
        partial(_moe_kernel, final_norm=final_norm),
        grid=(t // tm, n_e),
        in_specs=[pl.BlockSpec((tm, d), lambda i, e: (i, 0)),
                  pl.BlockSpec((None, 1, d, dff), lambda i, e: (layer, e, 0, 0)),
                  pl.BlockSpec((None, 1, d, dff), lambda i, e: (layer, e, 0, 0)),
                  pl.BlockSpec((None, 1, dff, d), lambda i, e: (layer, e, 0, 0)),
                  pl.BlockSpec((tm, LANES), lambda i, e: (i, 0)),
                  pl.BlockSpec((tm, d), lambda i, e: (i, 0)),
                  pl.BlockSpec((1, 1, d), lambda i, e: (i // tpb, 0, 0)),
                  pl.BlockSpec((1, d), lambda i, e: (0, 0))],
        out_specs=pl.BlockSpec((tm, d), lambda i, e: (i, 0)),
        out_shape=jax.ShapeDtypeStruct((t, d), F32),
        scratch_shapes=[pltpu.VMEM((tm, d), F32)],
        compiler_params=_cp(2),
        name="moe_dense",
    )(h2, wg, wu, wd, comb, x2, gt, g_final.reshape(1, d))


def _pad_heads(w, n_heads, width, start, stop, at=0):
    rows = w.shape[0]
    wh = w.reshape(rows, n_heads, width)[:, :, start:stop]
    out = jnp.zeros((rows, n_heads, LANES), w.dtype)
    out = out.at[:, :, at:at + (stop - start)].set(wh)
    return out.reshape(rows, n_heads * LANES)


def _rope_swap(w_rope):
    half = MLA_ROPE // 2
    return jnp.concatenate([-w_rope[..., half:], w_rope[..., :half]], axis=-1)


def _layer_weights(l, w_in, mla_q_norm_g, mla_w_q_b, mla_kv_norm_g, mla_w_kv_b,
                   w_branch_diff, w_branch_chunk, w_branch_mla, w_out):
    wi = w_in[l]
    o_chunk = 3 * DIFF_WIDTH
    o_mq = o_chunk + 3 * CHK_WIDTH
    o_mkv = o_mq + MLA_Q_LORA
    o_gate = o_mkv + MLA_KV_LORA + MLA_ROPE
    qk = MLA_NOPE + MLA_ROPE
    wqb = mla_w_q_b[l]
    wq_rope = wqb.reshape(MLA_Q_LORA, MLA_HEADS, qk)[:, :, MLA_NOPE:]
    wqs = jnp.zeros((MLA_Q_LORA, MLA_HEADS, LANES), F32).at[:, :, MLA_NOPE:qk].set(_rope_swap(wq_rope))
    wkr = wi[:, o_mkv + MLA_KV_LORA:o_gate]
    pad_kr = lambda a: jnp.zeros((D_MODEL, LANES), F32).at[:, MLA_NOPE:qk].set(a)
    wkvb = mla_w_kv_b[l]
    bf = lambda a: a.astype(BF16)

    def vt_weight(wv, n_heads, v_dim):
        padded = jnp.pad(wv.reshape(D_MODEL, n_heads, v_dim), ((0, 0), (0, 0), (0, ONES_ROWS)))
        return bf(padded.reshape(D_MODEL, n_heads * (v_dim + ONES_ROWS)).T)

    return {
        "diff_qk": bf(wi[:, :2 * DIFF_WIDTH]),
        "diff_vt": vt_weight(wi[:, 2 * DIFF_WIDTH:o_chunk], DIFF_HEADS, DIFF_V_DIM),
        "chunk_qk": bf(wi[:, o_chunk:o_chunk + 2 * CHK_WIDTH]),
        "chunk_vt": vt_weight(wi[:, o_chunk + 2 * CHK_WIDTH:o_mq], CHK_HEADS, CHK_HEAD_DIM),
        "gate": bf(wi[:, o_gate:]),
        "mq": bf(wi[:, o_mq:o_mkv]),
        "ckv": bf(wi[:, o_mkv:o_mkv + MLA_KV_LORA]),
        "kr": bf(pad_kr(wkr)),
        "krs": bf(pad_kr(_rope_swap(wkr))),
        "gq": mla_q_norm_g[l].reshape(1, -1),
        "gkv": mla_kv_norm_g[l].reshape(1, -1),
        "q": bf(_pad_heads(wqb, MLA_HEADS, qk, 0, qk)),
        "qs": bf(wqs.reshape(MLA_Q_LORA, MLA_HEADS * LANES)),
        "k": bf(_pad_heads(wkvb, MLA_HEADS, MLA_NOPE + MLA_V, 0, MLA_NOPE)),
        "vt": bf(jnp.pad(wkvb.reshape(MLA_KV_LORA, MLA_HEADS, MLA_NOPE + MLA_V)[:, :, MLA_NOPE:],
                         ((0, 0), (0, 0), (0, ONES_ROWS))).reshape(MLA_KV_LORA, MLA_HEADS * MLA_VT_ROWS).T),
        "br_diff": bf(w_branch_diff[l]),
        "br_chunk": bf(w_branch_chunk[l]),
        "br_mla": bf(w_branch_mla[l]),
        "out": bf(w_out[l]),
    }


def _rope_tables(seq):
    pos = jnp.arange(seq, dtype=F32)
    inv_freq = 1.0 / (ROPE_THETA ** (jnp.arange(0, MLA_ROPE, 2, dtype=F32) / MLA_ROPE))
    ang = pos[:, None] * inv_freq[None, :]
    cos, sin = jnp.cos(ang), jnp.sin(ang)
    qk = MLA_NOPE + MLA_ROPE
    cos_tab = jnp.zeros((seq, LANES), F32).at[:, :MLA_NOPE].set(1.0)
    cos_tab = cos_tab.at[:, MLA_NOPE:qk].set(jnp.concatenate([cos, cos], axis=-1))
    sin_tab = jnp.zeros((seq, LANES), F32).at[:, MLA_NOPE:qk].set(jnp.concatenate([sin, sin], axis=-1))
    return cos_tab, sin_tab


def kernel(x, c, w_mod, b_mod, g_norm_mix, g_norm_ffn, w_in, diff_lambda_q1, diff_lambda_k1, diff_lambda_q2, diff_lambda_k2, diff_subln_g, chunk_rel_bias, mla_q_norm_g, mla_w_q_b, mla_kv_norm_g, mla_w_kv_b, w_branch_diff, w_branch_chunk, w_branch_mla, w_out, w_router, router_bias, w_exp_gate, w_exp_up, w_exp_down, g_final):
    batch, seq, d = x.shape
    depth = w_in.shape[0]
    t = batch * seq
    x2 = x.reshape(t, d)
    cos_tab, sin_tab = _rope_tables(seq)
    mod = _modulation(c, w_mod, b_mod)
    chunk_bias = _chunk_bias(chunk_rel_bias)
    wr_pad = jnp.pad(w_router, ((0, 0), (0, LANES - N_EXPERTS)))
    wr_hi = wr_pad.astype(BF16)
    wr_lo = (wr_pad - wr_hi.astype(F32)).astype(BF16)
    for l in range(depth):
        sh1, sc1, gt1, sh2, sc2, gt2 = [m.reshape(batch, 1, d) for m in jnp.split(mod[l], 6, axis=-1)]
        w = _layer_weights(l, w_in, mla_q_norm_g, mla_w_q_b, mla_kv_norm_g, mla_w_kv_b,
                           w_branch_diff, w_branch_chunk, w_branch_mla, w_out)
        h = _norm_mod(x2, g_norm_mix[l], sc1, sh1, seq)
        lam_init = 0.8 - 0.6 * math.exp(-0.3 * l)
        dqk, dvt = _qk_vt(h, w["diff_qk"], w["diff_vt"], DIFF_V_DIM, "diff_prep")
        ya = _diff_attn(dqk, dvt, diff_lambda_q1[l], diff_lambda_k1[l], diff_lambda_q2[l], diff_lambda_k2[l],
                        diff_subln_g[l], batch, seq, lam_init)
        cqk, cvt = _qk_vt(h, w["chunk_qk"], w["chunk_vt"], CHK_HEAD_DIM, "chunk_prep")
        yb = _chunk_attn(cqk, cvt, chunk_bias, l, batch, seq)
        q, k, v = _mla_prep(h, w, cos_tab, sin_tab, seq)
        yc = _mla_attn(q, k, v, batch, seq)
        gates = _proj(h, w["gate"], sigmoid=True, name="proj_gate")
        x2, h2, comb = _merge(ya, yb, yc, gates, x2, gt1, sc2, sh2, g_norm_ffn[l], w, wr_hi, wr_lo,
                                router_bias, seq)
        x2 = _moe_dense(h2, comb, x2, gt2, w_exp_gate, w_exp_up, w_exp_down, l, g_final, l == depth - 1, seq)
    return x2.reshape(batch, seq, d)
```

```python
import math
from functools import partial

import jax
import jax.numpy as jnp
import numpy as np
from jax import lax
from jax.experimental import pallas as pl
from jax.experimental.pallas import tpu as pltpu

F32 = jnp.float32
BF16 = jnp.bfloat16

D_MODEL = 1024
CHUNK = 64
NORM_EPS = 1e-6
DIFF_HEADS = 4
DIFF_HEAD_DIM = 64
DIFF_V_DIM = 128
DIFF_WIDTH = 512
CHK_HEADS = 8
CHK_HEAD_DIM = 64
CHK_WIDTH = 512
CHK_LEFT_CHUNKS = 8
REL_CLIP = 256
MLA_HEADS = 8
MLA_Q_LORA = 384
MLA_KV_LORA = 256
MLA_NOPE = 64
MLA_ROPE = 32
MLA_V = 64
ROPE_THETA = 10000.0
N_EXPERTS = 16
N_GROUPS = 4
EXPERTS_PER_GROUP = 4
MOE_D_FF = 512

LANES = 128
NEG = -1e30
VMEM_LIMIT = 56 * 1024 * 1024

TM = 512
TKV = 256
ONES_ROWS = 16
MLA_VT_ROWS = MLA_V + ONES_ROWS
DIFF_VT_ROWS = DIFF_V_DIM + ONES_ROWS
LOG2E = math.log2(math.e)
CHK_VT_ROWS = CHK_HEAD_DIM + ONES_ROWS
CHK_WIN_TILES = CHK_LEFT_CHUNKS * CHUNK // TKV + 1
TM_MOE = 1024
MOE_EXPERTS_PER_STEP = 2


def _cp(n_axes):
    return pltpu.CompilerParams(dimension_semantics=("arbitrary",) * n_axes,
                                vmem_limit_bytes=VMEM_LIMIT)


def _rms(x, g):
    return x * lax.rsqrt(jnp.mean(x * x, axis=-1, keepdims=True) + NORM_EPS) * g


def _mod_kernel(c_ref, w_ref, b_ref, o_ref):
    c = c_ref[...]
    c_act = c * jax.nn.sigmoid(c)
    o_ref[0] = jnp.dot(c_act.astype(BF16), w_ref[0].astype(BF16),
                       preferred_element_type=F32) + b_ref[0]


def _modulation(c, w_mod, b_mod):
    n_layers, d, n = w_mod.shape
    b = c.shape[0]
    tn = 1024
    return pl.pallas_call(
        _mod_kernel,
        grid=(n_layers, n // tn),
        in_specs=[pl.BlockSpec((b, d), lambda l, j: (0, 0)),
                  pl.BlockSpec((1, d, tn), lambda l, j: (l, 0, j)),
                  pl.BlockSpec((1, 1, tn), lambda l, j: (l, 0, j))],
        out_specs=pl.BlockSpec((1, b, tn), lambda l, j: (l, 0, j)),
        out_shape=jax.ShapeDtypeStruct((n_layers, b, n), F32),
        compiler_params=_cp(2),
        name="adaln_mod",
    )(c, w_mod, b_mod.reshape(n_layers, 1, n))


def _norm_mod_kernel(x_ref, g_ref, sc_ref, sh_ref, o_ref):
    h = _rms(x_ref[...], g_ref[...]) * (1.0 + sc_ref[0]) + sh_ref[0]
    o_ref[...] = h.astype(o_ref.dtype)


def _norm_mod(x2, g, sc, sh, seq):
    t, d = x2.shape
    tpb = seq // TM
    bspec = pl.BlockSpec((1, 1, d), lambda i: (i // tpb, 0, 0))
    return pl.pallas_call(
        _norm_mod_kernel,
        grid=(t // TM,),
        in_specs=[pl.BlockSpec((TM, d), lambda i: (i, 0)),
                  pl.BlockSpec((1, d), lambda i: (0, 0)), bspec, bspec],
        out_specs=pl.BlockSpec((TM, d), lambda i: (i, 0)),
        out_shape=jax.ShapeDtypeStruct((t, d), BF16),
        compiler_params=_cp(1),
        name="norm_mod",
    )(x2, g.reshape(1, d), sc, sh)


def _proj_kernel(h_ref, w_ref, o_ref, *, tn, sigmoid):
    h = h_ref[...]
    for j in range(w_ref.shape[1] // tn):
        r = jnp.dot(h, w_ref[:, j * tn:(j + 1) * tn], preferred_element_type=F32)
        if sigmoid:
            r = jax.nn.sigmoid(r)
        o_ref[:, j * tn:(j + 1) * tn] = r.astype(o_ref.dtype)


def _proj(h, w, sigmoid=False, name="proj"):
    t, d = h.shape
    n = w.shape[1]
    return pl.pallas_call(
        partial(_proj_kernel, tn=512, sigmoid=sigmoid),
        grid=(t // TM,),
        in_specs=[pl.BlockSpec((TM, d), lambda i: (i, 0)),
                  pl.BlockSpec((d, n), lambda i: (0, 0))],
        out_specs=pl.BlockSpec((TM, n), lambda i: (i, 0)),
        out_shape=jax.ShapeDtypeStruct((t, n), BF16),
        compiler_params=_cp(1),
        name=name,
    )(h, w)


def _mla_prep_kernel(h_ref, wmq_ref, wckv_ref, wkr_ref, wkrs_ref, gq_ref, gkv_ref,
                     wq_ref, wqs_ref, wk_ref, wvt_ref, cos_ref, sin_ref,
                     q_ref, k_ref, vt_ref, *, scale):
    dot = partial(jnp.dot, preferred_element_type=F32)
    parts = [slice(j * TKV, (j + 1) * TKV) for j in range(vt_ref.shape[0])]
    low = [(dot(h_ref[r, :], wmq_ref[...]), dot(h_ref[r, :], wckv_ref[...]),
            dot(h_ref[r, :], wkr_ref[...]), dot(h_ref[r, :], wkrs_ref[...])) for r in parts]
    qn = [_rms(mq, gq_ref[...]).astype(BF16) for mq, _, _, _ in low]
    cn = [_rms(ckv, gkv_ref[...]).astype(BF16) for _, ckv, _, _ in low]
    kr = [a * cos_ref[r, :] + b * sin_ref[r, :] for r, (_, _, a, b) in zip(parts, low)]
    wide = [(dot(q, wq_ref[...]), dot(q, wqs_ref[...]), dot(c, wk_ref[...])) for q, c in zip(qn, cn)]
    for r, (qa, qb, ka), kr_r in zip(parts, wide, kr):
        cos = cos_ref[r, :]
        sin = sin_ref[r, :]
        for hd in range(MLA_HEADS):
            cols = slice(hd * LANES, (hd + 1) * LANES)
            q_ref[r, cols] = ((qa[:, cols] * cos + qb[:, cols] * sin) * scale).astype(q_ref.dtype)
            k_ref[r, cols] = (ka[:, cols] + kr_r).astype(k_ref.dtype)
    row = lax.broadcasted_iota(jnp.int32, (MLA_HEADS * MLA_VT_ROWS, 1), 0)
    ones_col = jnp.where(row % MLA_VT_ROWS >= MLA_V, 1.0, 0.0)
    for j, c in enumerate(cn):
        vt_ref[j] = (_nt_dot(wvt_ref[...], c) + ones_col).astype(vt_ref.dtype)


def _mla_prep(h, w, cos_tab, sin_tab, seq):
    t, d = h.shape
    tpb = seq // TM
    full = lambda a: pl.BlockSpec(a.shape, lambda i: (0,) * a.ndim)
    tab = pl.BlockSpec((TM, LANES), lambda i: (i % tpb, 0))
    ws = [w["mq"], w["ckv"], w["kr"], w["krs"], w["gq"], w["gkv"], w["q"], w["qs"], w["k"], w["vt"]]
    scale = (MLA_NOPE + MLA_ROPE) ** -0.5 * LOG2E
    vt_rows = MLA_HEADS * MLA_VT_ROWS
    return pl.pallas_call(
        partial(_mla_prep_kernel, scale=scale),
        grid=(t // TM,),
        in_specs=[pl.BlockSpec((TM, d), lambda i: (i, 0))] + [full(a) for a in ws] + [tab, tab],
        out_specs=[pl.BlockSpec((TM, MLA_HEADS * LANES), lambda i: (i, 0)),
                   pl.BlockSpec((TM, MLA_HEADS * LANES), lambda i: (i, 0)),
                   pl.BlockSpec((TM // TKV, vt_rows, TKV), lambda i: (i, 0, 0))],
        out_shape=[jax.ShapeDtypeStruct((t, MLA_HEADS * LANES), BF16),
                   jax.ShapeDtypeStruct((t, MLA_HEADS * LANES), BF16),
                   jax.ShapeDtypeStruct((t // TKV, vt_rows, TKV), BF16)],
        compiler_params=_cp(1),
        name="mla_prep",
    )(h, *ws, cos_tab, sin_tab)


def _nt_dot(a, b):
    return lax.dot_general(a, b, (((1,), (1,)), ((), ())), preferred_element_type=F32)


PIPE_DEPTH = 4


def _run_pipelined(units, scores, softmax, accumulate):
    n = len(units)
    pending = {i: scores(units[i]) for i in range(min(PIPE_DEPTH, n))}
    for i in range(n):
        alpha, p = softmax(units[i], pending.pop(i))
        if i + PIPE_DEPTH < n:
            pending[i + PIPE_DEPTH] = scores(units[i + PIPE_DEPTH])
        accumulate(units[i], alpha, p)


TILE_GROUP = 4


def _causal_tile_loop(qi, tiles):
    def body(j, carry):
        tiles([(TILE_GROUP * j + g, False) for g in range(TILE_GROUP)])
        return carry

    lax.fori_loop(0, qi // TILE_GROUP, body, 0)
    base = (qi // TILE_GROUP) * TILE_GROUP
    for rem in range(TILE_GROUP):
        pl.when(qi % TILE_GROUP == rem)(
            lambda rem=rem: tiles([(base + g, False) for g in range(rem)] + [(qi, True)]))


def _half_masks(rows):
    lane = lax.broadcasted_iota(jnp.int32, (rows, LANES), 1)
    lo = (lane < LANES // 2)
    return lo, jnp.logical_not(lo)


def _masked_halves(q, scale, lo, hi):
    qf = q.astype(F32) * scale
    return jnp.where(lo, qf, 0.0).astype(BF16), jnp.where(hi, qf, 0.0).astype(BF16)


def _qk_vt_kernel(h_ref, wqk_ref, wvt_ref, qk_ref, vt_ref, *, v_dim):
    h = h_ref[...]
    tn = 512
    for j in range(wqk_ref.shape[1] // tn):
        qk_ref[:, j * tn:(j + 1) * tn] = jnp.dot(h, wqk_ref[:, j * tn:(j + 1) * tn],
                                                 preferred_element_type=F32).astype(qk_ref.dtype)
    row = lax.broadcasted_iota(jnp.int32, (wvt_ref.shape[0], 1), 0)
    ones_col = jnp.where(row % (v_dim + ONES_ROWS) >= v_dim, 1.0, 0.0)
    vt = (_nt_dot(wvt_ref[...], h) + ones_col).astype(vt_ref.dtype)
    for j in range(vt_ref.shape[0]):
        vt_ref[j] = vt[:, j * TKV:(j + 1) * TKV]


def _qk_vt(h, wqk, wvt, v_dim, name):
    t, d = h.shape
    n = wqk.shape[1]
    vt_rows = wvt.shape[0]
    return pl.pallas_call(
        partial(_qk_vt_kernel, v_dim=v_dim),
        grid=(t // TM,),
        in_specs=[pl.BlockSpec((TM, d), lambda i: (i, 0)),
                  pl.BlockSpec((d, n), lambda i: (0, 0)),
                  pl.BlockSpec((vt_rows, d), lambda i: (0, 0))],
        out_specs=[pl.BlockSpec((TM, n), lambda i: (i, 0)),
                   pl.BlockSpec((TM // TKV, vt_rows, TKV), lambda i: (i, 0, 0))],
        out_shape=[jax.ShapeDtypeStruct((t, n), BF16),
                   jax.ShapeDtypeStruct((t // TKV, vt_rows, TKV), BF16)],
        compiler_params=_cp(1),
        name=name,
    )(h, wqk, wvt)


def _diff_bias_tiles():
    kj = np.arange(TKV)[:, None]
    qi = np.arange(TKV)[None, :]
    rel = (qi - kj).astype(np.float64)
    ok = (kj // CHUNK) <= (qi // CHUNK)
    off, diag = [], []
    for hd in range(DIFF_HEADS):
        c = 2.0 ** (-8.0 / DIFF_HEADS * (hd + 1)) * LOG2E
        off.append(-c * rel)
        diag.append(np.where(ok, -c * np.abs(rel), NEG))
    return np.stack([np.stack(off), np.stack(diag)]).astype(np.float32)


def _diff_attn_kernel(q_ref, k_ref, vt_ref, bias_ref, lq1_ref, lk1_ref, lq2_ref, lk2_ref, g_ref, o_ref,
                      qm_ref, m_ref, acc_ref, *, tq, lam_init):
    qi = pl.program_id(1)
    lo, hi = _half_masks(tq)
    scale = DIFF_HEAD_DIM ** -0.5 * LOG2E
    for hd in range(DIFF_HEADS):
        q0, q1 = _masked_halves(q_ref[:, hd * LANES:(hd + 1) * LANES], scale, lo, hi)
        qm_ref[2 * hd] = q0
        qm_ref[2 * hd + 1] = q1
    m_ref[...] = jnp.full(m_ref.shape, NEG, F32)
    acc_ref[...] = jnp.zeros(acc_ref.shape, F32)

    def scores(unit):
        kt, hm, diag = unit
        hd = hm // 2
        k = k_ref[pl.ds(pl.multiple_of(kt * tq, tq), tq), hd * LANES:(hd + 1) * LANES]
        return _nt_dot(k, qm_ref[hm]) + bias_ref[1 if diag else 0, hd]

    def softmax(unit, s):
        kt, hm, diag = unit
        c = 2.0 ** (-8.0 / DIFF_HEADS * (hm // 2 + 1)) * LOG2E
        shift = 0.0 if diag else (-c * tq) * (qi - kt).astype(F32)
        m_prev = m_ref[hm:hm + 1, :]
        m_new = jnp.maximum(m_prev, jnp.max(s, axis=0, keepdims=True) + shift)
        m_ref[hm:hm + 1, :] = m_new
        p = jnp.exp2(s - (m_new - shift)).astype(BF16)
        return jnp.exp2(m_prev - m_new), p

    def accumulate(unit, alpha, p):
        kt, hm, _ = unit
        hd = hm // 2
        pv = jnp.dot(vt_ref[kt, hd * DIFF_VT_ROWS:(hd + 1) * DIFF_VT_ROWS, :], p,
                     preferred_element_type=F32)
        acc_ref[hm] = alpha * acc_ref[hm] + pv

    def tiles(kts):
        _run_pipelined([(kt, hm, diag) for kt, diag in kts for hm in range(2 * DIFF_HEADS)],
                       scores, softmax, accumulate)

    _causal_tile_loop(qi, tiles)

    lam = (jnp.exp(jnp.sum(lq1_ref[...] * lk1_ref[...], axis=-1, keepdims=True))
           - jnp.exp(jnp.sum(lq2_ref[...] * lk2_ref[...], axis=-1, keepdims=True)) + lam_init)
    for hd in range(DIFF_HEADS):
        a0 = acc_ref[2 * hd]
        a1 = acc_ref[2 * hd + 1]
        o = (a0[0:DIFF_V_DIM, :] * (1.0 / a0[DIFF_V_DIM:DIFF_V_DIM + 1, :])
             - lam * (a1[0:DIFF_V_DIM, :] * (1.0 / a1[DIFF_V_DIM:DIFF_V_DIM + 1, :])))
        y = o * lax.rsqrt(jnp.mean(o * o, axis=0, keepdims=True) + NORM_EPS) * g_ref[...] * (1.0 - lam_init)
        o_ref[:, hd * LANES:(hd + 1) * LANES] = y.T.astype(o_ref.dtype)


def _diff_attn(qk, vt, lq1, lk1, lq2, lk2, g, batch, seq, lam_init):
    t = qk.shape[0]
    tq = TKV
    nq = seq // tq
    bias = _diff_bias_tiles()
    vec = lambda a: pl.BlockSpec(a.shape, lambda b, i: (0, 0))
    args = [a.reshape(1, -1) for a in (lq1, lk1, lq2, lk2)] + [g.reshape(-1, 1)]
    return pl.pallas_call(
        partial(_diff_attn_kernel, tq=tq, lam_init=lam_init),
        grid=(batch, nq),
        in_specs=[pl.BlockSpec((tq, DIFF_WIDTH), lambda b, i: (b * nq + i, 0)),
                  pl.BlockSpec((seq, DIFF_WIDTH), lambda b, i: (b, 1)),
                  pl.BlockSpec((nq,) + vt.shape[1:], lambda b, i: (b, 0, 0)),
                  pl.BlockSpec(bias.shape, lambda b, i: (0, 0, 0, 0))] + [vec(a) for a in args],
        out_specs=pl.BlockSpec((tq, DIFF_WIDTH), lambda b, i: (b * nq + i, 0)),
        out_shape=jax.ShapeDtypeStruct((t, DIFF_WIDTH), BF16),
        scratch_shapes=[pltpu.VMEM((2 * DIFF_HEADS, tq, LANES), BF16),
                        pltpu.VMEM((2 * DIFF_HEADS, tq), F32),
                        pltpu.VMEM((2 * DIFF_HEADS, DIFF_VT_ROWS, tq), F32)],
        compiler_params=_cp(2),
        name="diff_attn",
    )(qk, qk, vt, bias, *args)


def _chunk_attn_kernel(q_ref, k_ref, vt_ref, bias_ref, o_ref, qm_ref, stage_ref, *, tq):
    qi = pl.program_id(1)
    lo, hi = _half_masks(tq)
    scale = CHK_HEAD_DIM ** -0.5 * LOG2E
    for pair in range(CHK_HEADS // 2):
        q0, q1 = _masked_halves(q_ref[:, pair * LANES:(pair + 1) * LANES], scale, lo, hi)
        qm_ref[2 * pair] = q0
        qm_ref[2 * pair + 1] = q1

    def window(n_tiles):
        slots = range(CHK_WIN_TILES - n_tiles, CHK_WIN_TILES)

        def scores(hd):
            cols = slice((hd // 2) * LANES, (hd // 2 + 1) * LANES)
            out = []
            for w in slots:
                ks = pl.multiple_of((qi - (CHK_WIN_TILES - 1) + w) * tq, tq)
                out.append(_nt_dot(k_ref[pl.ds(ks, tq), cols], qm_ref[hd]) + bias_ref[hd, w])
            return out

        def softmax(hd, s_tiles):
            m = jnp.max(s_tiles[0], axis=0, keepdims=True)
            for s in s_tiles[1:]:
                m = jnp.maximum(m, jnp.max(s, axis=0, keepdims=True))
            return None, [jnp.exp2(s - m).astype(BF16) for s in s_tiles]

        def accumulate(hd, _, p_tiles):
            rows = slice(hd * CHK_VT_ROWS, (hd + 1) * CHK_VT_ROWS)
            acc = None
            for w, p in zip(slots, p_tiles):
                pv = jnp.dot(vt_ref[qi - (CHK_WIN_TILES - 1) + w, rows, :], p, preferred_element_type=F32)
                acc = pv if acc is None else acc + pv
            stage_ref[hd] = acc[0:CHK_HEAD_DIM, :] * (1.0 / acc[CHK_HEAD_DIM:CHK_HEAD_DIM + 1, :])

        _run_pipelined(list(range(CHK_HEADS)), scores, softmax, accumulate)

    for n_tiles in range(1, CHK_WIN_TILES):
        pl.when(qi == n_tiles - 1)(partial(window, n_tiles))
    pl.when(qi >= CHK_WIN_TILES - 1)(partial(window, CHK_WIN_TILES))

    for pair in range(CHK_HEADS // 2):
        both = jnp.concatenate([stage_ref[2 * pair], stage_ref[2 * pair + 1]], axis=0)
        o_ref[:, pair * LANES:(pair + 1) * LANES] = both.T.astype(o_ref.dtype)


def _chunk_attn(qk, vt, bias, layer, batch, seq):
    t = qk.shape[0]
    tq = TKV
    nq = seq // tq
    return pl.pallas_call(
        partial(_chunk_attn_kernel, tq=tq),
        grid=(batch, nq),
        in_specs=[pl.BlockSpec((tq, CHK_WIDTH), lambda b, i: (b * nq + i, 0)),
                  pl.BlockSpec((seq, CHK_WIDTH), lambda b, i: (b, 1)),
                  pl.BlockSpec((nq,) + vt.shape[1:], lambda b, i: (b, 0, 0)),
                  pl.BlockSpec((None,) + bias.shape[1:], lambda b, i: (layer, 0, 0, 0, 0))],
        out_specs=pl.BlockSpec((tq, CHK_WIDTH), lambda b, i: (b * nq + i, 0)),
        out_shape=jax.ShapeDtypeStruct((t, CHK_WIDTH), BF16),
        scratch_shapes=[pltpu.VMEM((CHK_HEADS, tq, LANES), BF16),
                        pltpu.VMEM((CHK_HEADS, CHK_HEAD_DIM, tq), F32)],
        compiler_params=_cp(2),
        name="chunk_attn",
    )(qk, qk, vt, bias)


def _chunk_bias(rel_tables):
    n_layers, n_heads, _ = rel_tables.shape
    period = 2 * TKV
    ext = jnp.concatenate([rel_tables[..., REL_CLIP - TKV:],
                           jnp.broadcast_to(rel_tables[..., 2 * REL_CLIP:], (n_layers, n_heads, 3 * TKV - 1 - REL_CLIP))],
                          axis=-1) * LOG2E
    seqs, bands = [], []
    kj = np.arange(TKV)[:, None]
    qi = np.arange(TKV)[None, :]
    for w in range(CHK_WIN_TILES):
        dist = (CHK_WIN_TILES - 1 - w) * TKV
        seqs.append(jnp.concatenate([ext[..., dist + TKV:dist + 2 * TKV], ext[..., dist:dist + TKV]], axis=-1))
        dchunk = (qi + dist) // CHUNK - kj // CHUNK
        bands.append((dchunk >= 0) & (dchunk <= CHK_LEFT_CHUNKS))
    seq = jnp.stack(seqs, axis=2).reshape(-1, period)
    flat = jnp.tile(seq, (1, TKV))[:, :TKV * (period - 1)]
    bias = flat.reshape(n_layers, n_heads, CHK_WIN_TILES, TKV, period - 1)[..., :TKV]
    return jnp.where(jnp.asarray(np.stack(bands)), bias, NEG).astype(F32)


def _mla_attn_kernel(q_ref, k_ref, vt_ref, o_ref, m_ref, acc_ref, *, tq):
    qi = pl.program_id(1)
    m_ref[...] = jnp.full(m_ref.shape, NEG, F32)
    acc_ref[...] = jnp.zeros(acc_ref.shape, F32)
    ki = lax.broadcasted_iota(jnp.int32, (tq, tq), 0)
    qj = lax.broadcasted_iota(jnp.int32, (tq, tq), 1)
    diag_ok = (ki // CHUNK) <= (qj // CHUNK)

    def scores(unit):
        kt, hd, masked = unit
        cols = slice(hd * LANES, (hd + 1) * LANES)
        s = _nt_dot(k_ref[pl.ds(pl.multiple_of(kt * tq, tq), tq), cols], q_ref[:, cols])
        return jnp.where(diag_ok, s, NEG) if masked else s

    def softmax_pv(unit, s):
        kt, hd, _ = unit
        m_prev = m_ref[hd:hd + 1, :]
        m_new = jnp.maximum(m_prev, jnp.max(s, axis=0, keepdims=True))
        m_ref[hd:hd + 1, :] = m_new
        p = jnp.exp2(s - m_new).astype(BF16)
        return jnp.exp2(m_prev - m_new), p

    def accumulate(unit, alpha, p):
        kt, hd, _ = unit
        pv = jnp.dot(vt_ref[kt, hd * MLA_VT_ROWS:(hd + 1) * MLA_VT_ROWS, :], p,
                     preferred_element_type=F32)
        acc_ref[hd] = alpha * acc_ref[hd] + pv

    def tiles(kts):
        _run_pipelined([(kt, hd, masked) for kt, masked in kts for hd in range(MLA_HEADS)],
                       scores, softmax_pv, accumulate)

    _causal_tile_loop(qi, tiles)

    for pair in range(MLA_HEADS // 2):
        halves = []
        for hd in (2 * pair, 2 * pair + 1):
            acc = acc_ref[hd]
            halves.append(acc[0:MLA_V, :] * (1.0 / acc[MLA_V:MLA_V + 1, :]))
        o_ref[:, pair * LANES:(pair + 1) * LANES] = jnp.concatenate(halves, axis=0).T.astype(o_ref.dtype)


def _mla_attn(q, k, vt, batch, seq):
    t = q.shape[0]
    tq = TKV
    nq = seq // tq
    return pl.pallas_call(
        partial(_mla_attn_kernel, tq=tq),
        grid=(batch, nq),
        in_specs=[pl.BlockSpec((tq, q.shape[1]), lambda b, i: (b * nq + i, 0)),
                  pl.BlockSpec((seq, k.shape[1]), lambda b, i: (b, 0)),
                  pl.BlockSpec((nq,) + vt.shape[1:], lambda b, i: (b, 0, 0))],
        out_specs=pl.BlockSpec((tq, MLA_HEADS * MLA_V), lambda b, i: (b * nq + i, 0)),
        out_shape=jax.ShapeDtypeStruct((t, MLA_HEADS * MLA_V), BF16),
        scratch_shapes=[pltpu.VMEM((MLA_HEADS, tq), F32),
                        pltpu.VMEM((MLA_HEADS, MLA_VT_ROWS, tq), F32)],
        compiler_params=_cp(2),
        name="mla_attn",
    )(q, k, vt)


def _route(logits_t, bias_ref):
    e_rows = [logits_t[e:e + 1, :] for e in range(N_EXPERTS)]
    scores = [jax.nn.sigmoid(r) for r in e_rows]
    sel = [scores[e] + bias_ref[e] for e in range(N_EXPERTS)]
    gscore = []
    for g in range(N_GROUPS):
        s4 = sel[g * EXPERTS_PER_GROUP:(g + 1) * EXPERTS_PER_GROUP]
        best = None
        for a in range(EXPERTS_PER_GROUP):
            for b in range(a + 1, EXPERTS_PER_GROUP):
                pair = s4[a] + s4[b]
                best = pair if best is None else jnp.maximum(best, pair)
        gscore.append(best)
    gbest = gscore[0]
    gidx = jnp.zeros_like(gbest, dtype=jnp.int32)
    for g in range(1, N_GROUPS):
        better = gscore[g] > gbest
        gbest = jnp.where(better, gscore[g], gbest)
        gidx = jnp.where(better, g, gidx)
    masked = [jnp.where(gidx == e // EXPERTS_PER_GROUP, sel[e], -jnp.inf) for e in range(N_EXPERTS)]

    def arg_first_max(vals):
        top = vals[0]
        for v in vals[1:]:
            top = jnp.maximum(top, v)
        idx = jnp.full(top.shape, N_EXPERTS, jnp.int32)
        for e in range(N_EXPERTS - 1, -1, -1):
            idx = jnp.where(vals[e] == top, e, idx)
        return idx

    i0 = arg_first_max(masked)
    i1 = arg_first_max([jnp.where(i0 == e, -jnp.inf, masked[e]) for e in range(N_EXPERTS)])
    w0 = sum(jnp.where(i0 == e, scores[e], 0.0) for e in range(N_EXPERTS))
    w1 = sum(jnp.where(i1 == e, scores[e], 0.0) for e in range(N_EXPERTS))
    den = w0 + w1
    w0 = w0 / den
    w1 = w1 / den
    rows = [jnp.where(i0 == e, w0, 0.0) + jnp.where(i1 == e, w1, 0.0) for e in range(N_EXPERTS)]
    return jnp.concatenate(rows, axis=0)


def _merge_kernel(ya_ref, yb_ref, yc_ref, gate_ref, x_ref, gt_ref, sc_ref, sh_ref, g_ref,
                  wa_ref, wb_ref, wc_ref, wo_ref, wrh_ref, wrl_ref, rb_ref,
                  xo_ref, h_ref, comb_ref):
    d = D_MODEL
    tm = x_ref.shape[0]
    halves = [slice(0, tm // 2), slice(tm // 2, tm)]
    dot = partial(jnp.dot, preferred_element_type=F32)
    branches = [(dot(ya_ref[r, :], wa_ref[...]), dot(yb_ref[r, :], wb_ref[...]), dot(yc_ref[r, :], wc_ref[...]))
                for r in halves]
    merged = [(gate_ref[r, 0:d].astype(F32) * a + gate_ref[r, d:2 * d].astype(F32) * b
               + gate_ref[r, 2 * d:3 * d].astype(F32) * c).astype(BF16)
              for r, (a, b, c) in zip(halves, branches)]
    ys = [dot(m, wo_ref[...]) for m in merged]
    h2s = []
    for r, y in zip(halves, ys):
        x_new = x_ref[r, :] + gt_ref[0] * y
        xo_ref[r, :] = x_new
        h2 = _rms(x_new, g_ref[...]) * (1.0 + sc_ref[0]) + sh_ref[0]
        h_ref[r, :] = h2.astype(h_ref.dtype)
        h2s.append(h2)
    logits = []
    for h2 in h2s:
        hi = h2.astype(BF16)
        lo = (h2 - hi.astype(F32)).astype(BF16)
        logits.append(dot(hi, wrh_ref[...]) + (dot(lo, wrh_ref[...]) + dot(hi, wrl_ref[...])))
    pad_rows = jnp.zeros((LANES - N_EXPERTS, tm // 2), F32)
    for r, lg in zip(halves, logits):
        comb_t = _route(lg.T[0:N_EXPERTS, :], rb_ref)
        comb_ref[r, :] = jnp.concatenate([comb_t, pad_rows], axis=0).T


def _merge(ya, yb, yc, gates, x2, gt, sc, sh, g, w, wr_hi, wr_lo, rbias, seq):
    t, d = x2.shape
    tm = TM
    tpb = seq // tm
    row = lambda n: pl.BlockSpec((tm, n), lambda i: (i, 0))
    full = lambda a: pl.BlockSpec(a.shape, lambda i: (0,) * a.ndim)
    bspec = pl.BlockSpec((1, 1, d), lambda i: (i // tpb, 0, 0))
    ws = [w["br_diff"], w["br_chunk"], w["br_mla"], w["out"], wr_hi, wr_lo]
    g2 = g.reshape(1, d)
    return pl.pallas_call(
        _merge_kernel,
        grid=(t // tm,),
        in_specs=[row(ya.shape[1]), row(yb.shape[1]), row(yc.shape[1]), row(gates.shape[1]), row(d),
                  bspec, bspec, bspec, full(g2)] + [full(a) for a in ws]
                 + [pl.BlockSpec(memory_space=pltpu.SMEM)],
        out_specs=[row(d), row(d), row(LANES)],
        out_shape=[jax.ShapeDtypeStruct((t, d), F32), jax.ShapeDtypeStruct((t, d), BF16),
                   jax.ShapeDtypeStruct((t, LANES), F32)],
        compiler_params=_cp(1),
        name="merge_route",
    )(ya, yb, yc, gates, x2, gt, sc, sh, g2, *ws, rbias)


def _moe_kernel(h_ref, wg_ref, wu_ref, wd_ref, comb_ref, x_ref, gt_ref, gf_ref, o_ref, acc_ref, *, final_norm):
    step = pl.program_id(1)
    n_sub, dff, d = wd_ref.shape

    @pl.when(step == 0)
    def _():
        acc_ref[...] = jnp.zeros_like(acc_ref)

    tm = h_ref.shape[0]
    halves = [slice(0, tm // 2), slice(tm // 2, tm)]
    dot = partial(jnp.dot, preferred_element_type=F32)
    lane = lax.broadcasted_iota(jnp.int32, comb_ref.shape, 1)
    comb = [jnp.sum(jnp.where(lane == step * n_sub + j, comb_ref[...], 0.0), axis=1, keepdims=True)
            for j in range(n_sub)]
    hidden = []
    for r in halves:
        parts = []
        for j in range(n_sub):
            a = dot(h_ref[r, :], wg_ref[j])
            b = dot(h_ref[r, :], wu_ref[j])
            parts.append((a * jax.nn.sigmoid(a) * b * comb[j][r, :]).astype(BF16))
        hidden.append(jnp.concatenate(parts, axis=1))
    wd = wd_ref[...].reshape(n_sub * dff, d)
    down = [dot(x, wd) for x in hidden]
    for r, y in zip(halves, down):
        acc_ref[r, :] += y

    @pl.when(step == pl.num_programs(1) - 1)
    def _():
        x_new = x_ref[...] + gt_ref[0] * acc_ref[...]
        o_ref[...] = _rms(x_new, gf_ref[...]) if final_norm else x_new


def _moe_dense(h2, comb, x2, gt, wg, wu, wd, layer, g_final, final_norm, seq):
    t, d = x2.shape
    tm = TM_MOE
    tpb = seq // tm
    _, n_e, _, dff = wg.shape
    n_sub = MOE_EXPERTS_PER_STEP
    return pl.pallas_call(
        partial(_moe_kernel, final_norm=final_norm),
        grid=(t // tm, n_e // n_sub),
        in_specs=[pl.BlockSpec((tm, d), lambda i, e: (i, 0)),
                  pl.BlockSpec((None, n_sub, d, dff), lambda i, e: (layer, e, 0, 0)),
                  pl.BlockSpec((None, n_sub, d, dff), lambda i, e: (layer, e, 0, 0)),
                  pl.BlockSpec((None, n_sub, dff, d), lambda i, e: (layer, e, 0, 0)),
                  pl.BlockSpec((tm, LANES), lambda i, e: (i, 0)),
                  pl.BlockSpec((tm, d), lambda i, e: (i, 0)),
                  pl.BlockSpec((1, 1, d), lambda i, e: (i // tpb, 0, 0)),
                  pl.BlockSpec((1, d), lambda i, e: (0, 0))],
        out_specs=pl.BlockSpec((tm, d), lambda i, e: (i, 0)),
        out_shape=jax.ShapeDtypeStruct((t, d), F32),
        scratch_shapes=[pltpu.VMEM((tm, d), F32)],
        compiler_params=_cp(2),
        name="moe_dense",
    )(h2, wg, wu, wd, comb, x2, gt, g_final.reshape(1, d))


def _pad_heads(w, n_heads, width, start, stop, at=0):
    rows = w.shape[0]
    wh = w.reshape(rows, n_heads, width)[:, :, start:stop]
    out = jnp.zeros((rows, n_heads, LANES), w.dtype)
    out = out.at[:, :, at:at + (stop - start)].set(wh)
    return out.reshape(rows, n_heads * LANES)


def _rope_swap(w_rope):
    half = MLA_ROPE // 2
    return jnp.concatenate([-w_rope[..., half:], w_rope[..., :half]], axis=-1)


def _layer_weights(l, w_in, mla_q_norm_g, mla_w_q_b, mla_kv_norm_g, mla_w_kv_b,
                   w_branch_diff, w_branch_chunk, w_branch_mla, w_out):
    wi = w_in[l]
    o_chunk = 3 * DIFF_WIDTH
    o_mq = o_chunk + 3 * CHK_WIDTH
    o_mkv = o_mq + MLA_Q_LORA
    o_gate = o_mkv + MLA_KV_LORA + MLA_ROPE
    qk = MLA_NOPE + MLA_ROPE
    wqb = mla_w_q_b[l]
    wq_rope = wqb.reshape(MLA_Q_LORA, MLA_HEADS, qk)[:, :, MLA_NOPE:]
    wqs = jnp.zeros((MLA_Q_LORA, MLA_HEADS, LANES), F32).at[:, :, MLA_NOPE:qk].set(_rope_swap(wq_rope))
    wkr = wi[:, o_mkv + MLA_KV_LORA:o_gate]
    pad_kr = lambda a: jnp.zeros((D_MODEL, LANES), F32).at[:, MLA_NOPE:qk].set(a)
    wkvb = mla_w_kv_b[l]
    bf = lambda a: a.astype(BF16)

    def vt_weight(wv, n_heads, v_dim):
        padded = jnp.pad(wv.reshape(D_MODEL, n_heads, v_dim), ((0, 0), (0, 0), (0, ONES_ROWS)))
        return bf(padded.reshape(D_MODEL, n_heads * (v_dim + ONES_ROWS)).T)

    return {
        "diff_qk": bf(wi[:, :2 * DIFF_WIDTH]),
        "diff_vt": vt_weight(wi[:, 2 * DIFF_WIDTH:o_chunk], DIFF_HEADS, DIFF_V_DIM),
        "chunk_qk": bf(wi[:, o_chunk:o_chunk + 2 * CHK_WIDTH]),
        "chunk_vt": vt_weight(wi[:, o_chunk + 2 * CHK_WIDTH:o_mq], CHK_HEADS, CHK_HEAD_DIM),
        "gate": bf(wi[:, o_gate:]),
        "mq": bf(wi[:, o_mq:o_mkv]),
        "ckv": bf(wi[:, o_mkv:o_mkv + MLA_KV_LORA]),
        "kr": bf(pad_kr(wkr)),
        "krs": bf(pad_kr(_rope_swap(wkr))),
        "gq": mla_q_norm_g[l].reshape(1, -1),
        "gkv": mla_kv_norm_g[l].reshape(1, -1),
        "q": bf(_pad_heads(wqb, MLA_HEADS, qk, 0, qk)),
        "qs": bf(wqs.reshape(MLA_Q_LORA, MLA_HEADS * LANES)),
        "k": bf(_pad_heads(wkvb, MLA_HEADS, MLA_NOPE + MLA_V, 0, MLA_NOPE)),
        "vt": bf(jnp.pad(wkvb.reshape(MLA_KV_LORA, MLA_HEADS, MLA_NOPE + MLA_V)[:, :, MLA_NOPE:],
                         ((0, 0), (0, 0), (0, ONES_ROWS))).reshape(MLA_KV_LORA, MLA_HEADS * MLA_VT_ROWS).T),
        "br_diff": bf(w_branch_diff[l]),
        "br_chunk": bf(w_branch_chunk[l]),
        "br_mla": bf(w_branch_mla[l]),
        "out": bf(w_out[l]),
    }


def _rope_tables(seq):
    pos = jnp.arange(seq, dtype=F32)
    inv_freq = 1.0 / (ROPE_THETA ** (jnp.arange(0, MLA_ROPE, 2, dtype=F32) / MLA_ROPE))
    ang = pos[:, None] * inv_freq[None, :]
    cos, sin = jnp.cos(ang), jnp.sin(ang)
    qk = MLA_NOPE + MLA_ROPE
    cos_tab = jnp.zeros((seq, LANES), F32).at[:, :MLA_NOPE].set(1.0)
    cos_tab = cos_tab.at[:, MLA_NOPE:qk].set(jnp.concatenate([cos, cos], axis=-1))
    sin_tab = jnp.zeros((seq, LANES), F32).at[:, MLA_NOPE:qk].set(jnp.concatenate([sin, sin], axis=-1))
    return cos_tab, sin_tab


def kernel(x, c, w_mod, b_mod, g_norm_mix, g_norm_ffn, w_in, diff_lambda_q1, diff_lambda_k1, diff_lambda_q2, diff_lambda_k2, diff_subln_g, chunk_rel_bias, mla_q_norm_g, mla_w_q_b, mla_kv_norm_g, mla_w_kv_b, w_branch_diff, w_branch_chunk, w_branch_mla, w_out, w_router, router_bias, w_exp_gate, w_exp_up, w_exp_down, g_final):
    batch, seq, d = x.shape
    depth = w_in.shape[0]
    t = batch * seq
    x2 = x.reshape(t, d)
    cos_tab, sin_tab = _rope_tables(seq)
    mod = _modulation(c, w_mod, b_mod)
    chunk_bias = _chunk_bias(chunk_rel_bias)
    wg_bf, wu_bf, wd_bf = (a.astype(BF16) for a in (w_exp_gate, w_exp_up, w_exp_down))
    wr_pad = jnp.pad(w_router, ((0, 0), (0, LANES - N_EXPERTS)))
    wr_hi = wr_pad.astype(BF16)
    wr_lo = (wr_pad - wr_hi.astype(F32)).astype(BF16)
    for l in range(depth):
        sh1, sc1, gt1, sh2, sc2, gt2 = [m.reshape(batch, 1, d) for m in jnp.split(mod[l], 6, axis=-1)]
        w = _layer_weights(l, w_in, mla_q_norm_g, mla_w_q_b, mla_kv_norm_g, mla_w_kv_b,
                           w_branch_diff, w_branch_chunk, w_branch_mla, w_out)
        h = _norm_mod(x2, g_norm_mix[l], sc1, sh1, seq)
        lam_init = 0.8 - 0.6 * math.exp(-0.3 * l)
        dqk, dvt = _qk_vt(h, w["diff_qk"], w["diff_vt"], DIFF_V_DIM, "diff_prep")
        ya = _diff_attn(dqk, dvt, diff_lambda_q1[l], diff_lambda_k1[l], diff_lambda_q2[l], diff_lambda_k2[l],
                        diff_subln_g[l], batch, seq, lam_init)
        cqk, cvt = _qk_vt(h, w["chunk_qk"], w["chunk_vt"], CHK_HEAD_DIM, "chunk_prep")
        yb = _chunk_attn(cqk, cvt, chunk_bias, l, batch, seq)
        q, k, v = _mla_prep(h, w, cos_tab, sin_tab, seq)
        yc = _mla_attn(q, k, v, batch, seq)
        gates = _proj(h, w["gate"], sigmoid=True, name="proj_gate")
        x2, h2, comb = _merge(ya, yb, yc, gates, x2, gt1, sc2, sh2, g_norm_ffn[l], w, wr_hi, wr_lo,
                                router_bias, seq)
        x2 = _moe_dense(h2, comb, x2, gt2, wg_bf, wu_bf, wd_bf, l, g_final, l == depth - 1, seq)
    return x2.reshape(batch, seq, d)
```

```python
import math
from functools import partial

import jax
import jax.numpy as jnp
import numpy as np
from jax import lax
from jax.experimental import pallas as pl
from jax.experimental.pallas import tpu as pltpu

F32 = jnp.float32
BF16 = jnp.bfloat16

D_MODEL = 1024
CHUNK = 64
NORM_EPS = 1e-6
DIFF_HEADS = 4
DIFF_HEAD_DIM = 64
DIFF_V_DIM = 128
DIFF_WIDTH = 512
CHK_HEADS = 8
CHK_HEAD_DIM = 64
CHK_WIDTH = 512
CHK_LEFT_CHUNKS = 8
REL_CLIP = 256
MLA_HEADS = 8
MLA_Q_LORA = 384
MLA_KV_LORA = 256
MLA_NOPE = 64
MLA_ROPE = 32
MLA_V = 64
ROPE_THETA = 10000.0
N_EXPERTS = 16
N_GROUPS = 4
EXPERTS_PER_GROUP = 4
MOE_D_FF = 512

LANES = 128
NEG = -1e30
VMEM_LIMIT = 56 * 1024 * 1024

TM = 512
TKV = 256
ONES_ROWS = 16
MLA_VT_ROWS = MLA_V + ONES_ROWS
DIFF_VT_ROWS = DIFF_V_DIM + ONES_ROWS
LOG2E = math.log2(math.e)
CHK_VT_ROWS = CHK_HEAD_DIM + ONES_ROWS
CHK_WIN_TILES = CHK_LEFT_CHUNKS * CHUNK // TKV + 1
TM_MOE = 1024
MOE_EXPERTS_PER_STEP = 2


def _cp(n_axes):
    return pltpu.CompilerParams(dimension_semantics=("arbitrary",) * n_axes,
                                vmem_limit_bytes=VMEM_LIMIT)


def _rms(x, g):
    return x * lax.rsqrt(jnp.mean(x * x, axis=-1, keepdims=True) + NORM_EPS) * g


def _mod_kernel(c_ref, w_ref, b_ref, o_ref):
    c = c_ref[...]
    c_act = c * jax.nn.sigmoid(c)
    o_ref[0] = jnp.dot(c_act.astype(BF16), w_ref[0].astype(BF16),
                       preferred_element_type=F32) + b_ref[0]


def _modulation(c, w_mod, b_mod):
    n_layers, d, n = w_mod.shape
    b = c.shape[0]
    tn = n // 2
    return pl.pallas_call(
        _mod_kernel,
        grid=(n_layers, n // tn),
        in_specs=[pl.BlockSpec((b, d), lambda l, j: (0, 0)),
                  pl.BlockSpec((1, d, tn), lambda l, j: (l, 0, j)),
                  pl.BlockSpec((1, 1, tn), lambda l, j: (l, 0, j))],
        out_specs=pl.BlockSpec((1, b, tn), lambda l, j: (l, 0, j)),
        out_shape=jax.ShapeDtypeStruct((n_layers, b, n), F32),
        compiler_params=_cp(2),
        name="adaln_mod",
    )(c, w_mod, b_mod.reshape(n_layers, 1, n))


def _norm_mod_kernel(x_ref, g_ref, sc_ref, sh_ref, o_ref):
    h = _rms(x_ref[...], g_ref[...]) * (1.0 + sc_ref[0]) + sh_ref[0]
    o_ref[...] = h.astype(o_ref.dtype)


def _norm_mod(x2, g, sc, sh, seq):
    t, d = x2.shape
    tpb = seq // TM
    bspec = pl.BlockSpec((1, 1, d), lambda i: (i // tpb, 0, 0))
    return pl.pallas_call(
        _norm_mod_kernel,
        grid=(t // TM,),
        in_specs=[pl.BlockSpec((TM, d), lambda i: (i, 0)),
                  pl.BlockSpec((1, d), lambda i: (0, 0)), bspec, bspec],
        out_specs=pl.BlockSpec((TM, d), lambda i: (i, 0)),
        out_shape=jax.ShapeDtypeStruct((t, d), BF16),
        compiler_params=_cp(1),
        name="norm_mod",
    )(x2, g.reshape(1, d), sc, sh)


def _mla_prep_kernel(h_ref, wmq_ref, wckv_ref, wkr_ref, wkrs_ref, gq_ref, gkv_ref,
                     wq_ref, wqs_ref, wk_ref, wvt_ref, cos_ref, sin_ref,
                     q_ref, k_ref, vt_ref, *, scale):
    dot = partial(jnp.dot, preferred_element_type=F32)
    parts = [slice(j * TKV, (j + 1) * TKV) for j in range(vt_ref.shape[0])]
    low = [(dot(h_ref[r, :], wmq_ref[...]), dot(h_ref[r, :], wckv_ref[...]),
            dot(h_ref[r, :], wkr_ref[...]), dot(h_ref[r, :], wkrs_ref[...])) for r in parts]
    qn = [_rms(mq, gq_ref[...]).astype(BF16) for mq, _, _, _ in low]
    cn = [_rms(ckv, gkv_ref[...]).astype(BF16) for _, ckv, _, _ in low]
    kr = [a * cos_ref[r, :] + b * sin_ref[r, :] for r, (_, _, a, b) in zip(parts, low)]
    wide = [(dot(q, wq_ref[...]), dot(q, wqs_ref[...]), dot(c, wk_ref[...])) for q, c in zip(qn, cn)]
    for r, (qa, qb, ka), kr_r in zip(parts, wide, kr):
        cos = cos_ref[r, :]
        sin = sin_ref[r, :]
        for hd in range(MLA_HEADS):
            cols = slice(hd * LANES, (hd + 1) * LANES)
            q_ref[r, cols] = ((qa[:, cols] * cos + qb[:, cols] * sin) * scale).astype(q_ref.dtype)
            k_ref[r, cols] = (ka[:, cols] + kr_r).astype(k_ref.dtype)
    row = lax.broadcasted_iota(jnp.int32, (MLA_HEADS * MLA_VT_ROWS, 1), 0)
    ones_col = jnp.where(row % MLA_VT_ROWS >= MLA_V, 1.0, 0.0)
    for j, c in enumerate(cn):
        vt_ref[j] = (_nt_dot(wvt_ref[...], c) + ones_col).astype(vt_ref.dtype)


def _mla_prep(h, w, cos_tab, sin_tab, seq):
    t, d = h.shape
    tpb = seq // TM
    full = lambda a: pl.BlockSpec(a.shape, lambda i: (0,) * a.ndim)
    tab = pl.BlockSpec((TM, LANES), lambda i: (i % tpb, 0))
    ws = [w["mq"], w["ckv"], w["kr"], w["krs"], w["gq"], w["gkv"], w["q"], w["qs"], w["k"], w["vt"]]
    scale = (MLA_NOPE + MLA_ROPE) ** -0.5 * LOG2E
    vt_rows = MLA_HEADS * MLA_VT_ROWS
    return pl.pallas_call(
        partial(_mla_prep_kernel, scale=scale),
        grid=(t // TM,),
        in_specs=[pl.BlockSpec((TM, d), lambda i: (i, 0))] + [full(a) for a in ws] + [tab, tab],
        out_specs=[pl.BlockSpec((TM, MLA_HEADS * LANES), lambda i: (i, 0)),
                   pl.BlockSpec((TM, MLA_HEADS * LANES), lambda i: (i, 0)),
                   pl.BlockSpec((TM // TKV, vt_rows, TKV), lambda i: (i, 0, 0))],
        out_shape=[jax.ShapeDtypeStruct((t, MLA_HEADS * LANES), BF16),
                   jax.ShapeDtypeStruct((t, MLA_HEADS * LANES), BF16),
                   jax.ShapeDtypeStruct((t // TKV, vt_rows, TKV), BF16)],
        compiler_params=_cp(1),
        name="mla_prep",
    )(h, *ws, cos_tab, sin_tab)


def _nt_dot(a, b):
    return lax.dot_general(a, b, (((1,), (1,)), ((), ())), preferred_element_type=F32)


PIPE_DEPTH = 4


def _run_pipelined(units, scores, softmax, accumulate):
    n = len(units)
    pending = {i: scores(units[i]) for i in range(min(PIPE_DEPTH, n))}
    for i in range(n):
        alpha, p = softmax(units[i], pending.pop(i))
        if i + PIPE_DEPTH < n:
            pending[i + PIPE_DEPTH] = scores(units[i + PIPE_DEPTH])
        accumulate(units[i], alpha, p)


TILE_GROUP = 4


def _causal_tile_loop(qi, tiles):
    def body(j, carry):
        tiles([(TILE_GROUP * j + g, False) for g in range(TILE_GROUP)])
        return carry

    lax.fori_loop(0, qi // TILE_GROUP, body, 0)
    base = (qi // TILE_GROUP) * TILE_GROUP
    for rem in range(TILE_GROUP):
        pl.when(qi % TILE_GROUP == rem)(
            lambda rem=rem: tiles([(base + g, False) for g in range(rem)] + [(qi, True)]))


def _half_masks(rows):
    lane = lax.broadcasted_iota(jnp.int32, (rows, LANES), 1)
    lo = (lane < LANES // 2)
    return lo, jnp.logical_not(lo)


def _masked_halves(q, scale, lo, hi):
    qf = q.astype(F32) * scale
    return jnp.where(lo, qf, 0.0).astype(BF16), jnp.where(hi, qf, 0.0).astype(BF16)


def _qk_vt_kernel(h_ref, wqk_ref, wvt_ref, qk_ref, vt_ref, *, v_dim):
    h = h_ref[...]
    tn = 512
    for j in range(wqk_ref.shape[1] // tn):
        qk_ref[:, j * tn:(j + 1) * tn] = jnp.dot(h, wqk_ref[:, j * tn:(j + 1) * tn],
                                                 preferred_element_type=F32).astype(qk_ref.dtype)
    row = lax.broadcasted_iota(jnp.int32, (wvt_ref.shape[0], 1), 0)
    ones_col = jnp.where(row % (v_dim + ONES_ROWS) >= v_dim, 1.0, 0.0)
    vt = (_nt_dot(wvt_ref[...], h) + ones_col).astype(vt_ref.dtype)
    for j in range(vt_ref.shape[0]):
        vt_ref[j] = vt[:, j * TKV:(j + 1) * TKV]


def _qk_vt(h, wqk, wvt, v_dim, name):
    t, d = h.shape
    n = wqk.shape[1]
    vt_rows = wvt.shape[0]
    return pl.pallas_call(
        partial(_qk_vt_kernel, v_dim=v_dim),
        grid=(t // TM,),
        in_specs=[pl.BlockSpec((TM, d), lambda i: (i, 0)),
                  pl.BlockSpec((d, n), lambda i: (0, 0)),
                  pl.BlockSpec((vt_rows, d), lambda i: (0, 0))],
        out_specs=[pl.BlockSpec((TM, n), lambda i: (i, 0)),
                   pl.BlockSpec((TM // TKV, vt_rows, TKV), lambda i: (i, 0, 0))],
        out_shape=[jax.ShapeDtypeStruct((t, n), BF16),
                   jax.ShapeDtypeStruct((t // TKV, vt_rows, TKV), BF16)],
        compiler_params=_cp(1),
        name=name,
    )(h, wqk, wvt)


def _diff_bias_tiles():
    kj = np.arange(TKV)[:, None]
    qi = np.arange(TKV)[None, :]
    rel = (qi - kj).astype(np.float64)
    ok = (kj // CHUNK) <= (qi // CHUNK)
    off, diag = [], []
    for hd in range(DIFF_HEADS):
        c = 2.0 ** (-8.0 / DIFF_HEADS * (hd + 1)) * LOG2E
        off.append(-c * rel)
        diag.append(np.where(ok, -c * np.abs(rel), NEG))
    return np.stack([np.stack(off), np.stack(diag)]).astype(np.float32)


def _diff_attn_kernel(q_ref, k_ref, vt_ref, bias_ref, lq1_ref, lk1_ref, lq2_ref, lk2_ref, g_ref, o_ref,
                      qm_ref, m_ref, acc_ref, *, tq, lam_init):
    qi = pl.program_id(1)
    lo, hi = _half_masks(tq)
    scale = DIFF_HEAD_DIM ** -0.5 * LOG2E
    for hd in range(DIFF_HEADS):
        q0, q1 = _masked_halves(q_ref[:, hd * LANES:(hd + 1) * LANES], scale, lo, hi)
        qm_ref[2 * hd] = q0
        qm_ref[2 * hd + 1] = q1
    m_ref[...] = jnp.full(m_ref.shape, NEG, F32)
    acc_ref[...] = jnp.zeros(acc_ref.shape, F32)

    def scores(unit):
        kt, hm, diag = unit
        hd = hm // 2
        k = k_ref[pl.ds(pl.multiple_of(kt * tq, tq), tq), hd * LANES:(hd + 1) * LANES]
        return _nt_dot(k, qm_ref[hm]) + bias_ref[1 if diag else 0, hd]

    def softmax(unit, s):
        kt, hm, diag = unit
        c = 2.0 ** (-8.0 / DIFF_HEADS * (hm // 2 + 1)) * LOG2E
        shift = 0.0 if diag else (-c * tq) * (qi - kt).astype(F32)
        m_prev = m_ref[hm:hm + 1, :]
        m_new = jnp.maximum(m_prev, jnp.max(s, axis=0, keepdims=True) + shift)
        m_ref[hm:hm + 1, :] = m_new
        p = jnp.exp2(s - (m_new - shift)).astype(BF16)
        return jnp.exp2(m_prev - m_new), p

    def accumulate(unit, alpha, p):
        kt, hm, _ = unit
        hd = hm // 2
        pv = jnp.dot(vt_ref[kt, hd * DIFF_VT_ROWS:(hd + 1) * DIFF_VT_ROWS, :], p,
                     preferred_element_type=F32)
        acc_ref[hm] = alpha * acc_ref[hm] + pv

    def tiles(kts):
        _run_pipelined([(kt, hm, diag) for kt, diag in kts for hm in range(2 * DIFF_HEADS)],
                       scores, softmax, accumulate)

    _causal_tile_loop(qi, tiles)

    lam = (jnp.exp(jnp.sum(lq1_ref[...] * lk1_ref[...], axis=-1, keepdims=True))
           - jnp.exp(jnp.sum(lq2_ref[...] * lk2_ref[...], axis=-1, keepdims=True)) + lam_init)
    for hd in range(DIFF_HEADS):
        a0 = acc_ref[2 * hd]
        a1 = acc_ref[2 * hd + 1]
        o = (a0[0:DIFF_V_DIM, :] * (1.0 / a0[DIFF_V_DIM:DIFF_V_DIM + 1, :])
             - lam * (a1[0:DIFF_V_DIM, :] * (1.0 / a1[DIFF_V_DIM:DIFF_V_DIM + 1, :])))
        y = o * lax.rsqrt(jnp.mean(o * o, axis=0, keepdims=True) + NORM_EPS) * g_ref[...] * (1.0 - lam_init)
        o_ref[:, hd * LANES:(hd + 1) * LANES] = y.T.astype(o_ref.dtype)


def _diff_attn(qk, vt, lq1, lk1, lq2, lk2, g, batch, seq, lam_init):
    t = qk.shape[0]
    tq = TKV
    nq = seq // tq
    bias = _diff_bias_tiles()
    vec = lambda a: pl.BlockSpec(a.shape, lambda b, i: (0, 0))
    args = [a.reshape(1, -1) for a in (lq1, lk1, lq2, lk2)] + [g.reshape(-1, 1)]
    return pl.pallas_call(
        partial(_diff_attn_kernel, tq=tq, lam_init=lam_init),
        grid=(batch, nq),
        in_specs=[pl.BlockSpec((tq, DIFF_WIDTH), lambda b, i: (b * nq + i, 0)),
                  pl.BlockSpec((seq, DIFF_WIDTH), lambda b, i: (b, 1)),
                  pl.BlockSpec((nq,) + vt.shape[1:], lambda b, i: (b, 0, 0)),
                  pl.BlockSpec(bias.shape, lambda b, i: (0, 0, 0, 0))] + [vec(a) for a in args],
        out_specs=pl.BlockSpec((tq, DIFF_WIDTH), lambda b, i: (b * nq + i, 0)),
        out_shape=jax.ShapeDtypeStruct((t, DIFF_WIDTH), BF16),
        scratch_shapes=[pltpu.VMEM((2 * DIFF_HEADS, tq, LANES), BF16),
                        pltpu.VMEM((2 * DIFF_HEADS, tq), F32),
                        pltpu.VMEM((2 * DIFF_HEADS, DIFF_VT_ROWS, tq), F32)],
        compiler_params=_cp(2),
        name="diff_attn",
    )(qk, qk, vt, bias, *args)


def _chunk_attn_kernel(q_ref, k_ref, vt_ref, bias_ref, o_ref, qm_ref, stage_ref, *, tq):
    qi = pl.program_id(1)
    lo, hi = _half_masks(tq)
    scale = CHK_HEAD_DIM ** -0.5 * LOG2E
    for pair in range(CHK_HEADS // 2):
        q0, q1 = _masked_halves(q_ref[:, pair * LANES:(pair + 1) * LANES], scale, lo, hi)
        qm_ref[2 * pair] = q0
        qm_ref[2 * pair + 1] = q1

    def window(n_tiles):
        slots = range(CHK_WIN_TILES - n_tiles, CHK_WIN_TILES)

        def scores(hd):
            cols = slice((hd // 2) * LANES, (hd // 2 + 1) * LANES)
            out = []
            for w in slots:
                ks = pl.multiple_of((qi - (CHK_WIN_TILES - 1) + w) * tq, tq)
                out.append(_nt_dot(k_ref[pl.ds(ks, tq), cols], qm_ref[hd]) + bias_ref[hd, w])
            return out

        def softmax(hd, s_tiles):
            m = jnp.max(s_tiles[0], axis=0, keepdims=True)
            for s in s_tiles[1:]:
                m = jnp.maximum(m, jnp.max(s, axis=0, keepdims=True))
            return None, [jnp.exp2(s - m).astype(BF16) for s in s_tiles]

        def accumulate(hd, _, p_tiles):
            rows = slice(hd * CHK_VT_ROWS, (hd + 1) * CHK_VT_ROWS)
            acc = None
            for w, p in zip(slots, p_tiles):
                pv = jnp.dot(vt_ref[qi - (CHK_WIN_TILES - 1) + w, rows, :], p, preferred_element_type=F32)
                acc = pv if acc is None else acc + pv
            stage_ref[hd] = acc[0:CHK_HEAD_DIM, :] * (1.0 / acc[CHK_HEAD_DIM:CHK_HEAD_DIM + 1, :])

        _run_pipelined(list(range(CHK_HEADS)), scores, softmax, accumulate)

    for n_tiles in range(1, CHK_WIN_TILES):
        pl.when(qi == n_tiles - 1)(partial(window, n_tiles))
    pl.when(qi >= CHK_WIN_TILES - 1)(partial(window, CHK_WIN_TILES))

    for pair in range(CHK_HEADS // 2):
        both = jnp.concatenate([stage_ref[2 * pair], stage_ref[2 * pair + 1]], axis=0)
        o_ref[:, pair * LANES:(pair + 1) * LANES] = both.T.astype(o_ref.dtype)


def _chunk_attn(qk, vt, bias, layer, batch, seq):
    t = qk.shape[0]
    tq = TKV
    nq = seq // tq
    return pl.pallas_call(
        partial(_chunk_attn_kernel, tq=tq),
        grid=(batch, nq),
        in_specs=[pl.BlockSpec((tq, CHK_WIDTH), lambda b, i: (b * nq + i, 0)),
                  pl.BlockSpec((seq, CHK_WIDTH), lambda b, i: (b, 1)),
                  pl.BlockSpec((nq,) + vt.shape[1:], lambda b, i: (b, 0, 0)),
                  pl.BlockSpec((None,) + bias.shape[1:], lambda b, i: (layer, 0, 0, 0, 0))],
        out_specs=pl.BlockSpec((tq, CHK_WIDTH), lambda b, i: (b * nq + i, 0)),
        out_shape=jax.ShapeDtypeStruct((t, CHK_WIDTH), BF16),
        scratch_shapes=[pltpu.VMEM((CHK_HEADS, tq, LANES), BF16),
                        pltpu.VMEM((CHK_HEADS, CHK_HEAD_DIM, tq), F32)],
        compiler_params=_cp(2),
        name="chunk_attn",
    )(qk, qk, vt, bias)


def _chunk_bias(rel_tables):
    n_layers, n_heads, _ = rel_tables.shape
    period = 2 * TKV
    ext = jnp.concatenate([rel_tables[..., REL_CLIP - TKV:],
                           jnp.broadcast_to(rel_tables[..., 2 * REL_CLIP:], (n_layers, n_heads, 3 * TKV - 1 - REL_CLIP))],
                          axis=-1) * LOG2E
    seqs, bands = [], []
    kj = np.arange(TKV)[:, None]
    qi = np.arange(TKV)[None, :]
    for w in range(CHK_WIN_TILES):
        dist = (CHK_WIN_TILES - 1 - w) * TKV
        seqs.append(jnp.concatenate([ext[..., dist + TKV:dist + 2 * TKV], ext[..., dist:dist + TKV]], axis=-1))
        dchunk = (qi + dist) // CHUNK - kj // CHUNK
        bands.append((dchunk >= 0) & (dchunk <= CHK_LEFT_CHUNKS))
    seq = jnp.stack(seqs, axis=2).reshape(-1, period)
    flat = jnp.tile(seq, (1, TKV))[:, :TKV * (period - 1)]
    bias = flat.reshape(n_layers, n_heads, CHK_WIN_TILES, TKV, period - 1)[..., :TKV]
    return jnp.where(jnp.asarray(np.stack(bands)), bias, NEG).astype(F32)


def _mla_attn_kernel(q_ref, k_ref, vt_ref, o_ref, m_ref, acc_ref, *, tq):
    qi = pl.program_id(1)
    m_ref[...] = jnp.full(m_ref.shape, NEG, F32)
    acc_ref[...] = jnp.zeros(acc_ref.shape, F32)
    ki = lax.broadcasted_iota(jnp.int32, (tq, tq), 0)
    qj = lax.broadcasted_iota(jnp.int32, (tq, tq), 1)
    diag_ok = (ki // CHUNK) <= (qj // CHUNK)

    def scores(unit):
        kt, hd, masked = unit
        cols = slice(hd * LANES, (hd + 1) * LANES)
        s = _nt_dot(k_ref[pl.ds(pl.multiple_of(kt * tq, tq), tq), cols], q_ref[:, cols])
        return jnp.where(diag_ok, s, NEG) if masked else s

    def softmax_pv(unit, s):
        kt, hd, _ = unit
        m_prev = m_ref[hd:hd + 1, :]
        m_new = jnp.maximum(m_prev, jnp.max(s, axis=0, keepdims=True))
        m_ref[hd:hd + 1, :] = m_new
        p = jnp.exp2(s - m_new).astype(BF16)
        return jnp.exp2(m_prev - m_new), p

    def accumulate(unit, alpha, p):
        kt, hd, _ = unit
        pv = jnp.dot(vt_ref[kt, hd * MLA_VT_ROWS:(hd + 1) * MLA_VT_ROWS, :], p,
                     preferred_element_type=F32)
        acc_ref[hd] = alpha * acc_ref[hd] + pv

    def tiles(kts):
        _run_pipelined([(kt, hd, masked) for kt, masked in kts for hd in range(MLA_HEADS)],
                       scores, softmax_pv, accumulate)

    _causal_tile_loop(qi, tiles)

    for pair in range(MLA_HEADS // 2):
        halves = []
        for hd in (2 * pair, 2 * pair + 1):
            acc = acc_ref[hd]
            halves.append(acc[0:MLA_V, :] * (1.0 / acc[MLA_V:MLA_V + 1, :]))
        o_ref[:, pair * LANES:(pair + 1) * LANES] = jnp.concatenate(halves, axis=0).T.astype(o_ref.dtype)


def _mla_attn(q, k, vt, batch, seq):
    t = q.shape[0]
    tq = TKV
    nq = seq // tq
    return pl.pallas_call(
        partial(_mla_attn_kernel, tq=tq),
        grid=(batch, nq),
        in_specs=[pl.BlockSpec((tq, q.shape[1]), lambda b, i: (b * nq + i, 0)),
                  pl.BlockSpec((seq, k.shape[1]), lambda b, i: (b, 0)),
                  pl.BlockSpec((nq,) + vt.shape[1:], lambda b, i: (b, 0, 0))],
        out_specs=pl.BlockSpec((tq, MLA_HEADS * MLA_V), lambda b, i: (b * nq + i, 0)),
        out_shape=jax.ShapeDtypeStruct((t, MLA_HEADS * MLA_V), BF16),
        scratch_shapes=[pltpu.VMEM((MLA_HEADS, tq), F32),
                        pltpu.VMEM((MLA_HEADS, MLA_VT_ROWS, tq), F32)],
        compiler_params=_cp(2),
        name="mla_attn",
    )(q, k, vt)


def _route(logits_t, bias_ref):
    e_rows = [logits_t[e:e + 1, :] for e in range(N_EXPERTS)]
    scores = [jax.nn.sigmoid(r) for r in e_rows]
    sel = [scores[e] + bias_ref[e] for e in range(N_EXPERTS)]
    gscore = []
    for g in range(N_GROUPS):
        s4 = sel[g * EXPERTS_PER_GROUP:(g + 1) * EXPERTS_PER_GROUP]
        best = None
        for a in range(EXPERTS_PER_GROUP):
            for b in range(a + 1, EXPERTS_PER_GROUP):
                pair = s4[a] + s4[b]
                best = pair if best is None else jnp.maximum(best, pair)
        gscore.append(best)
    gbest = gscore[0]
    gidx = jnp.zeros_like(gbest, dtype=jnp.int32)
    for g in range(1, N_GROUPS):
        better = gscore[g] > gbest
        gbest = jnp.where(better, gscore[g], gbest)
        gidx = jnp.where(better, g, gidx)
    masked = [jnp.where(gidx == e // EXPERTS_PER_GROUP, sel[e], -jnp.inf) for e in range(N_EXPERTS)]

    def arg_first_max(vals):
        top = vals[0]
        for v in vals[1:]:
            top = jnp.maximum(top, v)
        idx = jnp.full(top.shape, N_EXPERTS, jnp.int32)
        for e in range(N_EXPERTS - 1, -1, -1):
            idx = jnp.where(vals[e] == top, e, idx)
        return idx

    i0 = arg_first_max(masked)
    i1 = arg_first_max([jnp.where(i0 == e, -jnp.inf, masked[e]) for e in range(N_EXPERTS)])
    w0 = sum(jnp.where(i0 == e, scores[e], 0.0) for e in range(N_EXPERTS))
    w1 = sum(jnp.where(i1 == e, scores[e], 0.0) for e in range(N_EXPERTS))
    den = w0 + w1
    w0 = w0 / den
    w1 = w1 / den
    rows = [jnp.where(i0 == e, w0, 0.0) + jnp.where(i1 == e, w1, 0.0) for e in range(N_EXPERTS)]
    return jnp.concatenate(rows, axis=0)


def _merge_kernel(ya_ref, yb_ref, yc_ref, hm_ref, x_ref, gt_ref, sc_ref, sh_ref, g_ref,
                  wa_ref, wb_ref, wc_ref, wo_ref, wgate_ref, wrh_ref, wrl_ref, rb_ref,
                  xo_ref, h_ref, comb_ref):
    d = D_MODEL
    tm = x_ref.shape[0]
    halves = [slice(0, tm // 2), slice(tm // 2, tm)]
    dot = partial(jnp.dot, preferred_element_type=F32)
    merged = []
    for r in halves:
        branch = (dot(ya_ref[r, :], wa_ref[...]), dot(yb_ref[r, :], wb_ref[...]), dot(yc_ref[r, :], wc_ref[...]))
        total = None
        for k, y in enumerate(branch):
            term = jax.nn.sigmoid(dot(hm_ref[r, :], wgate_ref[:, k * d:(k + 1) * d])) * y
            total = term if total is None else total + term
        merged.append(total.astype(BF16))
    ys = [dot(m, wo_ref[...]) for m in merged]
    h2s = []
    for r, y in zip(halves, ys):
        x_new = x_ref[r, :] + gt_ref[0] * y
        xo_ref[r, :] = x_new
        h2 = _rms(x_new, g_ref[...]) * (1.0 + sc_ref[0]) + sh_ref[0]
        h_ref[r, :] = h2.astype(h_ref.dtype)
        h2s.append(h2)
    logits = []
    for h2 in h2s:
        hi = h2.astype(BF16)
        lo = (h2 - hi.astype(F32)).astype(BF16)
        logits.append(dot(hi, wrh_ref[...]) + (dot(lo, wrh_ref[...]) + dot(hi, wrl_ref[...])))
    pad_rows = jnp.zeros((LANES - N_EXPERTS, tm // 2), F32)
    for r, lg in zip(halves, logits):
        comb_t = _route(lg.T[0:N_EXPERTS, :], rb_ref)
        comb_ref[r, :] = jnp.concatenate([comb_t, pad_rows], axis=0).T


def _merge(ya, yb, yc, h, x2, gt, sc, sh, g, w, wr_hi, wr_lo, rbias, seq):
    t, d = x2.shape
    tm = TM
    tpb = seq // tm
    row = lambda n: pl.BlockSpec((tm, n), lambda i: (i, 0))
    full = lambda a: pl.BlockSpec(a.shape, lambda i: (0,) * a.ndim)
    bspec = pl.BlockSpec((1, 1, d), lambda i: (i // tpb, 0, 0))
    ws = [w["br_diff"], w["br_chunk"], w["br_mla"], w["out"], w["gate"], wr_hi, wr_lo]
    g2 = g.reshape(1, d)
    return pl.pallas_call(
        _merge_kernel,
        grid=(t // tm,),
        in_specs=[row(ya.shape[1]), row(yb.shape[1]), row(yc.shape[1]), row(h.shape[1]), row(d),
                  bspec, bspec, bspec, full(g2)] + [full(a) for a in ws]
                 + [pl.BlockSpec(memory_space=pltpu.SMEM)],
        out_specs=[row(d), row(d), row(LANES)],
        out_shape=[jax.ShapeDtypeStruct((t, d), F32), jax.ShapeDtypeStruct((t, d), BF16),
                   jax.ShapeDtypeStruct((t, LANES), F32)],
        compiler_params=_cp(1),
        name="merge_route",
    )(ya, yb, yc, h, x2, gt, sc, sh, g2, *ws, rbias)


def _moe_kernel(h_ref, wg_ref, wu_ref, wd_ref, comb_ref, x_ref, gt_ref, gf_ref, o_ref, acc_ref, *, final_norm):
    step = pl.program_id(1)
    n_sub, dff, d = wd_ref.shape

    @pl.when(step == 0)
    def _():
        acc_ref[...] = jnp.zeros_like(acc_ref)

    tm = h_ref.shape[0]
    halves = [slice(0, tm // 2), slice(tm // 2, tm)]
    dot = partial(jnp.dot, preferred_element_type=F32)
    lane = lax.broadcasted_iota(jnp.int32, comb_ref.shape, 1)
    comb = [jnp.sum(jnp.where(lane == step * n_sub + j, comb_ref[...], 0.0), axis=1, keepdims=True)
            for j in range(n_sub)]
    hidden = []
    for r in halves:
        parts = []
        for j in range(n_sub):
            a = dot(h_ref[r, :], wg_ref[j])
            b = dot(h_ref[r, :], wu_ref[j])
            parts.append((a * jax.nn.sigmoid(a) * b * comb[j][r, :]).astype(BF16))
        hidden.append(jnp.concatenate(parts, axis=1))
    wd = wd_ref[...].reshape(n_sub * dff, d)
    down = [dot(x, wd) for x in hidden]
    for r, y in zip(halves, down):
        acc_ref[r, :] += y

    @pl.when(step == pl.num_programs(1) - 1)
    def _():
        x_new = x_ref[...] + gt_ref[0] * acc_ref[...]
        o_ref[...] = _rms(x_new, gf_ref[...]) if final_norm else x_new


def _moe_dense(h2, comb, x2, gt, wg, wu, wd, layer, g_final, final_norm, seq):
    t, d = x2.shape
    tm = TM_MOE
    tpb = seq // tm
    _, n_e, _, dff = wg.shape
    n_sub = MOE_EXPERTS_PER_STEP
    return pl.pallas_call(
        partial(_moe_kernel, final_norm=final_norm),
        grid=(t // tm, n_e // n_sub),
        in_specs=[pl.BlockSpec((tm, d), lambda i, e: (i, 0)),
                  pl.BlockSpec((None, n_sub, d, dff), lambda i, e: (layer, e, 0, 0)),
                  pl.BlockSpec((None, n_sub, d, dff), lambda i, e: (layer, e, 0, 0)),
                  pl.BlockSpec((None, n_sub, dff, d), lambda i, e: (layer, e, 0, 0)),
                  pl.BlockSpec((tm, LANES), lambda i, e: (i, 0)),
                  pl.BlockSpec((tm, d), lambda i, e: (i, 0)),
                  pl.BlockSpec((1, 1, d), lambda i, e: (i // tpb, 0, 0)),
                  pl.BlockSpec((1, d), lambda i, e: (0, 0))],
        out_specs=pl.BlockSpec((tm, d), lambda i, e: (i, 0)),
        out_shape=jax.ShapeDtypeStruct((t, d), F32),
        scratch_shapes=[pltpu.VMEM((tm, d), F32)],
        compiler_params=_cp(2),
        name="moe_dense",
    )(h2, wg, wu, wd, comb, x2, gt, g_final.reshape(1, d))


def _pad_heads(w, n_heads, width, start, stop, at=0):
    rows = w.shape[0]
    wh = w.reshape(rows, n_heads, width)[:, :, start:stop]
    out = jnp.zeros((rows, n_heads, LANES), w.dtype)
    out = out.at[:, :, at:at + (stop - start)].set(wh)
    return out.reshape(rows, n_heads * LANES)


def _rope_swap(w_rope):
    half = MLA_ROPE // 2
    return jnp.concatenate([-w_rope[..., half:], w_rope[..., :half]], axis=-1)


def _layer_weights(l, w_in, mla_q_norm_g, mla_w_q_b, mla_kv_norm_g, mla_w_kv_b,
                   w_branch_diff, w_branch_chunk, w_branch_mla, w_out):
    wi = w_in[l]
    o_chunk = 3 * DIFF_WIDTH
    o_mq = o_chunk + 3 * CHK_WIDTH
    o_mkv = o_mq + MLA_Q_LORA
    o_gate = o_mkv + MLA_KV_LORA + MLA_ROPE
    qk = MLA_NOPE + MLA_ROPE
    wqb = mla_w_q_b[l]
    wq_rope = wqb.reshape(MLA_Q_LORA, MLA_HEADS, qk)[:, :, MLA_NOPE:]
    wqs = jnp.zeros((MLA_Q_LORA, MLA_HEADS, LANES), F32).at[:, :, MLA_NOPE:qk].set(_rope_swap(wq_rope))
    wkr = wi[:, o_mkv + MLA_KV_LORA:o_gate]
    pad_kr = lambda a: jnp.zeros((D_MODEL, LANES), F32).at[:, MLA_NOPE:qk].set(a)
    wkvb = mla_w_kv_b[l]
    bf = lambda a: a.astype(BF16)

    def vt_weight(wv, n_heads, v_dim):
        padded = jnp.pad(wv.reshape(D_MODEL, n_heads, v_dim), ((0, 0), (0, 0), (0, ONES_ROWS)))
        return bf(padded.reshape(D_MODEL, n_heads * (v_dim + ONES_ROWS)).T)

    return {
        "diff_qk": bf(wi[:, :2 * DIFF_WIDTH]),
        "diff_vt": vt_weight(wi[:, 2 * DIFF_WIDTH:o_chunk], DIFF_HEADS, DIFF_V_DIM),
        "chunk_qk": bf(wi[:, o_chunk:o_chunk + 2 * CHK_WIDTH]),
        "chunk_vt": vt_weight(wi[:, o_chunk + 2 * CHK_WIDTH:o_mq], CHK_HEADS, CHK_HEAD_DIM),
        "gate": bf(wi[:, o_gate:]),
        "mq": bf(wi[:, o_mq:o_mkv]),
        "ckv": bf(wi[:, o_mkv:o_mkv + MLA_KV_LORA]),
        "kr": bf(pad_kr(wkr)),
        "krs": bf(pad_kr(_rope_swap(wkr))),
        "gq": mla_q_norm_g[l].reshape(1, -1),
        "gkv": mla_kv_norm_g[l].reshape(1, -1),
        "q": bf(_pad_heads(wqb, MLA_HEADS, qk, 0, qk)),
        "qs": bf(wqs.reshape(MLA_Q_LORA, MLA_HEADS * LANES)),
        "k": bf(_pad_heads(wkvb, MLA_HEADS, MLA_NOPE + MLA_V, 0, MLA_NOPE)),
        "vt": bf(jnp.pad(wkvb.reshape(MLA_KV_LORA, MLA_HEADS, MLA_NOPE + MLA_V)[:, :, MLA_NOPE:],
                         ((0, 0), (0, 0), (0, ONES_ROWS))).reshape(MLA_KV_LORA, MLA_HEADS * MLA_VT_ROWS).T),
        "br_diff": bf(w_branch_diff[l]),
        "br_chunk": bf(w_branch_chunk[l]),
        "br_mla": bf(w_branch_mla[l]),
        "out": bf(w_out[l]),
    }


def _rope_tables(seq):
    pos = jnp.arange(seq, dtype=F32)
    inv_freq = 1.0 / (ROPE_THETA ** (jnp.arange(0, MLA_ROPE, 2, dtype=F32) / MLA_ROPE))
    ang = pos[:, None] * inv_freq[None, :]
    cos, sin = jnp.cos(ang), jnp.sin(ang)
    qk = MLA_NOPE + MLA_ROPE
    cos_tab = jnp.zeros((seq, LANES), F32).at[:, :MLA_NOPE].set(1.0)
    cos_tab = cos_tab.at[:, MLA_NOPE:qk].set(jnp.concatenate([cos, cos], axis=-1))
    sin_tab = jnp.zeros((seq, LANES), F32).at[:, MLA_NOPE:qk].set(jnp.concatenate([sin, sin], axis=-1))
    return cos_tab, sin_tab


def kernel(x, c, w_mod, b_mod, g_norm_mix, g_norm_ffn, w_in, diff_lambda_q1, diff_lambda_k1, diff_lambda_q2, diff_lambda_k2, diff_subln_g, chunk_rel_bias, mla_q_norm_g, mla_w_q_b, mla_kv_norm_g, mla_w_kv_b, w_branch_diff, w_branch_chunk, w_branch_mla, w_out, w_router, router_bias, w_exp_gate, w_exp_up, w_exp_down, g_final):
    batch, seq, d = x.shape
    depth = w_in.shape[0]
    t = batch * seq
    x2 = x.reshape(t, d)
    cos_tab, sin_tab = _rope_tables(seq)
    mod = _modulation(c, w_mod, b_mod)
    chunk_bias = _chunk_bias(chunk_rel_bias)
    wg_bf, wu_bf, wd_bf = (a.astype(BF16) for a in (w_exp_gate, w_exp_up, w_exp_down))
    wr_pad = jnp.pad(w_router, ((0, 0), (0, LANES - N_EXPERTS)))
    wr_hi = wr_pad.astype(BF16)
    wr_lo = (wr_pad - wr_hi.astype(F32)).astype(BF16)
    for l in range(depth):
        sh1, sc1, gt1, sh2, sc2, gt2 = [m.reshape(batch, 1, d) for m in jnp.split(mod[l], 6, axis=-1)]
        w = _layer_weights(l, w_in, mla_q_norm_g, mla_w_q_b, mla_kv_norm_g, mla_w_kv_b,
                           w_branch_diff, w_branch_chunk, w_branch_mla, w_out)
        h = _norm_mod(x2, g_norm_mix[l], sc1, sh1, seq)
        lam_init = 0.8 - 0.6 * math.exp(-0.3 * l)
        dqk, dvt = _qk_vt(h, w["diff_qk"], w["diff_vt"], DIFF_V_DIM, "diff_prep")
        ya = _diff_attn(dqk, dvt, diff_lambda_q1[l], diff_lambda_k1[l], diff_lambda_q2[l], diff_lambda_k2[l],
                        diff_subln_g[l], batch, seq, lam_init)
        cqk, cvt = _qk_vt(h, w["chunk_qk"], w["chunk_vt"], CHK_HEAD_DIM, "chunk_prep")
        yb = _chunk_attn(cqk, cvt, chunk_bias, l, batch, seq)
        q, k, v = _mla_prep(h, w, cos_tab, sin_tab, seq)
        yc = _mla_attn(q, k, v, batch, seq)
        x2, h2, comb = _merge(ya, yb, yc, h, x2, gt1, sc2, sh2, g_norm_ffn[l], w, wr_hi, wr_lo,
                                router_bias, seq)
        x2 = _moe_dense(h2, comb, x2, gt2, wg_bf, wu_bf, wd_bf, l, g_final, l == depth - 1, seq)
    return x2.reshape(batch, seq, d)
```

```python
import math
from functools import partial

import jax
import jax.numpy as jnp
import numpy as np
from jax import lax
from jax.experimental import pallas as pl
from jax.experimental.pallas import tpu as pltpu

F32 = jnp.float32
BF16 = jnp.bfloat16

D_MODEL = 1024
CHUNK = 64
NORM_EPS = 1e-6
DIFF_HEADS = 4
DIFF_HEAD_DIM = 64
DIFF_V_DIM = 128
DIFF_WIDTH = 512
CHK_HEADS = 8
CHK_HEAD_DIM = 64
CHK_WIDTH = 512
CHK_LEFT_CHUNKS = 8
REL_CLIP = 256
MLA_HEADS = 8
MLA_Q_LORA = 384
MLA_KV_LORA = 256
MLA_NOPE = 64
MLA_ROPE = 32
MLA_V = 64
ROPE_THETA = 10000.0
N_EXPERTS = 16
N_GROUPS = 4
EXPERTS_PER_GROUP = 4
MOE_D_FF = 512

LANES = 128
NEG = -1e30
VMEM_LIMIT = 56 * 1024 * 1024

TM = 512
TKV = 256
ONES_ROWS = 16
MLA_VT_ROWS = MLA_V + ONES_ROWS
DIFF_VT_ROWS = DIFF_V_DIM + ONES_ROWS
LOG2E = math.log2(math.e)
CHK_VT_ROWS = CHK_HEAD_DIM + ONES_ROWS
CHK_WIN_TILES = CHK_LEFT_CHUNKS * CHUNK // TKV + 1
TM_MOE = 1024
MOE_EXPERTS_PER_STEP = 2


def _cp(n_axes):
    return pltpu.CompilerParams(dimension_semantics=("arbitrary",) * n_axes,
                                vmem_limit_bytes=VMEM_LIMIT)


def _rms(x, g):
    return x * lax.rsqrt(jnp.mean(x * x, axis=-1, keepdims=True) + NORM_EPS) * g


def _mod_kernel(c_ref, w_ref, b_ref, o_ref):
    c = c_ref[...]
    c_act = c * jax.nn.sigmoid(c)
    o_ref[0] = jnp.dot(c_act.astype(BF16), w_ref[0].astype(BF16),
                       preferred_element_type=F32) + b_ref[0]


def _modulation(c, w_mod, b_mod):
    n_layers, d, n = w_mod.shape
    b = c.shape[0]
    tn = n // 2
    return pl.pallas_call(
        _mod_kernel,
        grid=(n_layers, n // tn),
        in_specs=[pl.BlockSpec((b, d), lambda l, j: (0, 0)),
                  pl.BlockSpec((1, d, tn), lambda l, j: (l, 0, j)),
                  pl.BlockSpec((1, 1, tn), lambda l, j: (l, 0, j))],
        out_specs=pl.BlockSpec((1, b, tn), lambda l, j: (l, 0, j)),
        out_shape=jax.ShapeDtypeStruct((n_layers, b, n), F32),
        compiler_params=_cp(2),
        name="adaln_mod",
    )(c, w_mod, b_mod.reshape(n_layers, 1, n))


def _norm_mod_kernel(x_ref, g_ref, sc_ref, sh_ref, o_ref):
    h = _rms(x_ref[...], g_ref[...]) * (1.0 + sc_ref[0]) + sh_ref[0]
    o_ref[...] = h.astype(o_ref.dtype)


def _norm_mod(x2, g, sc, sh, seq):
    t, d = x2.shape
    tpb = seq // TM
    bspec = pl.BlockSpec((1, 1, d), lambda i: (i // tpb, 0, 0))
    return pl.pallas_call(
        _norm_mod_kernel,
        grid=(t // TM,),
        in_specs=[pl.BlockSpec((TM, d), lambda i: (i, 0)),
                  pl.BlockSpec((1, d), lambda i: (0, 0)), bspec, bspec],
        out_specs=pl.BlockSpec((TM, d), lambda i: (i, 0)),
        out_shape=jax.ShapeDtypeStruct((t, d), BF16),
        compiler_params=_cp(1),
        name="norm_mod",
    )(x2, g.reshape(1, d), sc, sh)


def _mla_prep_kernel(h_ref, wmq_ref, wckv_ref, wkr_ref, wkrs_ref, gq_ref, gkv_ref,
                     wq_ref, wqs_ref, wk_ref, wvt_ref, cos_ref, sin_ref,
                     q_ref, k_ref, vt_ref, *, scale):
    dot = partial(jnp.dot, preferred_element_type=F32)
    parts = [slice(j * TKV, (j + 1) * TKV) for j in range(vt_ref.shape[0])]
    low = [(dot(h_ref[r, :], wmq_ref[...]), dot(h_ref[r, :], wckv_ref[...]),
            dot(h_ref[r, :], wkr_ref[...]), dot(h_ref[r, :], wkrs_ref[...])) for r in parts]
    qn = [_rms(mq, gq_ref[...]).astype(BF16) for mq, _, _, _ in low]
    cn = [_rms(ckv, gkv_ref[...]).astype(BF16) for _, ckv, _, _ in low]
    kr = [a * cos_ref[r, :] + b * sin_ref[r, :] for r, (_, _, a, b) in zip(parts, low)]
    wide = [(dot(q, wq_ref[...]), dot(q, wqs_ref[...]), dot(c, wk_ref[...])) for q, c in zip(qn, cn)]
    for r, (qa, qb, ka), kr_r in zip(parts, wide, kr):
        cos = cos_ref[r, :]
        sin = sin_ref[r, :]
        for hd in range(MLA_HEADS):
            cols = slice(hd * LANES, (hd + 1) * LANES)
            q_ref[r, cols] = ((qa[:, cols] * cos + qb[:, cols] * sin) * scale).astype(q_ref.dtype)
            k_ref[r, cols] = (ka[:, cols] + kr_r).astype(k_ref.dtype)
    row = lax.broadcasted_iota(jnp.int32, (MLA_HEADS * MLA_VT_ROWS, 1), 0)
    ones_col = jnp.where(row % MLA_VT_ROWS >= MLA_V, 1.0, 0.0)
    for j, c in enumerate(cn):
        vt_ref[j] = (_nt_dot(wvt_ref[...], c) + ones_col).astype(vt_ref.dtype)


def _mla_prep(h, w, cos_tab, sin_tab, seq):
    t, d = h.shape
    tpb = seq // TM
    full = lambda a: pl.BlockSpec(a.shape, lambda i: (0,) * a.ndim)
    tab = pl.BlockSpec((TM, LANES), lambda i: (i % tpb, 0))
    ws = [w["mq"], w["ckv"], w["kr"], w["krs"], w["gq"], w["gkv"], w["q"], w["qs"], w["k"], w["vt"]]
    scale = (MLA_NOPE + MLA_ROPE) ** -0.5 * LOG2E
    vt_rows = MLA_HEADS * MLA_VT_ROWS
    return pl.pallas_call(
        partial(_mla_prep_kernel, scale=scale),
        grid=(t // TM,),
        in_specs=[pl.BlockSpec((TM, d), lambda i: (i, 0))] + [full(a) for a in ws] + [tab, tab],
        out_specs=[pl.BlockSpec((TM, MLA_HEADS * LANES), lambda i: (i, 0)),
                   pl.BlockSpec((TM, MLA_HEADS * LANES), lambda i: (i, 0)),
                   pl.BlockSpec((TM // TKV, vt_rows, TKV), lambda i: (i, 0, 0))],
        out_shape=[jax.ShapeDtypeStruct((t, MLA_HEADS * LANES), BF16),
                   jax.ShapeDtypeStruct((t, MLA_HEADS * LANES), BF16),
                   jax.ShapeDtypeStruct((t // TKV, vt_rows, TKV), BF16)],
        compiler_params=_cp(1),
        name="mla_prep",
    )(h, *ws, cos_tab, sin_tab)


def _nt_dot(a, b):
    return lax.dot_general(a, b, (((1,), (1,)), ((), ())), preferred_element_type=F32)


PIPE_DEPTH = 4


def _run_pipelined(units, scores, softmax, accumulate):
    n = len(units)
    pending = {i: scores(units[i]) for i in range(min(PIPE_DEPTH, n))}
    for i in range(n):
        alpha, p = softmax(units[i], pending.pop(i))
        if i + PIPE_DEPTH < n:
            pending[i + PIPE_DEPTH] = scores(units[i + PIPE_DEPTH])
        accumulate(units[i], alpha, p)


TILE_GROUP = 8


def _causal_tile_loop(qi, tiles):
    def body(j, carry):
        tiles([(TILE_GROUP * j + g, False) for g in range(TILE_GROUP)])
        return carry

    lax.fori_loop(0, qi // TILE_GROUP, body, 0)
    base = (qi // TILE_GROUP) * TILE_GROUP
    for rem in range(TILE_GROUP):
        pl.when(qi % TILE_GROUP == rem)(
            lambda rem=rem: tiles([(base + g, False) for g in range(rem)] + [(qi, True)]))


def _half_masks(rows):
    lane = lax.broadcasted_iota(jnp.int32, (rows, LANES), 1)
    lo = (lane < LANES // 2)
    return lo, jnp.logical_not(lo)


def _masked_halves(q, scale, lo, hi):
    qf = q.astype(F32) * scale
    return jnp.where(lo, qf, 0.0).astype(BF16), jnp.where(hi, qf, 0.0).astype(BF16)


def _qk_vt_kernel(h_ref, wqk_ref, wvt_ref, qk_ref, vt_ref, *, v_dim):
    h = h_ref[...]
    tn = 512
    for j in range(wqk_ref.shape[1] // tn):
        qk_ref[:, j * tn:(j + 1) * tn] = jnp.dot(h, wqk_ref[:, j * tn:(j + 1) * tn],
                                                 preferred_element_type=F32).astype(qk_ref.dtype)
    row = lax.broadcasted_iota(jnp.int32, (wvt_ref.shape[0], 1), 0)
    ones_col = jnp.where(row % (v_dim + ONES_ROWS) >= v_dim, 1.0, 0.0)
    vt = (_nt_dot(wvt_ref[...], h) + ones_col).astype(vt_ref.dtype)
    for j in range(vt_ref.shape[0]):
        vt_ref[j] = vt[:, j * TKV:(j + 1) * TKV]


def _qk_vt(h, wqk, wvt, v_dim, name):
    t, d = h.shape
    n = wqk.shape[1]
    vt_rows = wvt.shape[0]
    return pl.pallas_call(
        partial(_qk_vt_kernel, v_dim=v_dim),
        grid=(t // TM,),
        in_specs=[pl.BlockSpec((TM, d), lambda i: (i, 0)),
                  pl.BlockSpec((d, n), lambda i: (0, 0)),
                  pl.BlockSpec((vt_rows, d), lambda i: (0, 0))],
        out_specs=[pl.BlockSpec((TM, n), lambda i: (i, 0)),
                   pl.BlockSpec((TM // TKV, vt_rows, TKV), lambda i: (i, 0, 0))],
        out_shape=[jax.ShapeDtypeStruct((t, n), BF16),
                   jax.ShapeDtypeStruct((t // TKV, vt_rows, TKV), BF16)],
        compiler_params=_cp(1),
        name=name,
    )(h, wqk, wvt)


def _diff_bias_tiles():
    kj = np.arange(TKV)[:, None]
    qi = np.arange(TKV)[None, :]
    rel = (qi - kj).astype(np.float64)
    ok = (kj // CHUNK) <= (qi // CHUNK)
    off, diag = [], []
    for hd in range(DIFF_HEADS):
        c = 2.0 ** (-8.0 / DIFF_HEADS * (hd + 1)) * LOG2E
        off.append(-c * rel)
        diag.append(np.where(ok, -c * np.abs(rel), NEG))
    return np.stack([np.stack(off), np.stack(diag)]).astype(np.float32)


def _diff_attn_kernel(q_ref, k_ref, vt_ref, bias_ref, lq1_ref, lk1_ref, lq2_ref, lk2_ref, g_ref, o_ref,
                      qm_ref, m_ref, acc_ref, *, tq, lam_init):
    qi = pl.program_id(1)
    lo, hi = _half_masks(tq)
    scale = DIFF_HEAD_DIM ** -0.5 * LOG2E
    for hd in range(DIFF_HEADS):
        q0, q1 = _masked_halves(q_ref[:, hd * LANES:(hd + 1) * LANES], scale, lo, hi)
        qm_ref[2 * hd] = q0
        qm_ref[2 * hd + 1] = q1
    m_ref[...] = jnp.full(m_ref.shape, NEG, F32)
    acc_ref[...] = jnp.zeros(acc_ref.shape, F32)

    def scores(unit):
        kt, hm, diag = unit
        hd = hm // 2
        k = k_ref[pl.ds(pl.multiple_of(kt * tq, tq), tq), hd * LANES:(hd + 1) * LANES]
        return _nt_dot(k, qm_ref[hm]) + bias_ref[1 if diag else 0, hd]

    def softmax(unit, s):
        kt, hm, diag = unit
        c = 2.0 ** (-8.0 / DIFF_HEADS * (hm // 2 + 1)) * LOG2E
        shift = 0.0 if diag else (-c * tq) * (qi - kt).astype(F32)
        m_prev = m_ref[hm:hm + 1, :]
        m_new = jnp.maximum(m_prev, jnp.max(s, axis=0, keepdims=True) + shift)
        m_ref[hm:hm + 1, :] = m_new
        p = jnp.exp2(s - (m_new - shift)).astype(BF16)
        return jnp.exp2(m_prev - m_new), p

    def accumulate(unit, alpha, p):
        kt, hm, _ = unit
        hd = hm // 2
        pv = jnp.dot(vt_ref[kt, hd * DIFF_VT_ROWS:(hd + 1) * DIFF_VT_ROWS, :], p,
                     preferred_element_type=F32)
        acc_ref[hm] = alpha * acc_ref[hm] + pv

    def tiles(kts):
        _run_pipelined([(kt, hm, diag) for kt, diag in kts for hm in range(2 * DIFF_HEADS)],
                       scores, softmax, accumulate)

    _causal_tile_loop(qi, tiles)

    lam = (jnp.exp(jnp.sum(lq1_ref[...] * lk1_ref[...], axis=-1, keepdims=True))
           - jnp.exp(jnp.sum(lq2_ref[...] * lk2_ref[...], axis=-1, keepdims=True)) + lam_init)
    for hd in range(DIFF_HEADS):
        a0 = acc_ref[2 * hd]
        a1 = acc_ref[2 * hd + 1]
        o = (a0[0:DIFF_V_DIM, :] * (1.0 / a0[DIFF_V_DIM:DIFF_V_DIM + 1, :])
             - lam * (a1[0:DIFF_V_DIM, :] * (1.0 / a1[DIFF_V_DIM:DIFF_V_DIM + 1, :])))
        y = o * lax.rsqrt(jnp.mean(o * o, axis=0, keepdims=True) + NORM_EPS) * g_ref[...] * (1.0 - lam_init)
        o_ref[:, hd * LANES:(hd + 1) * LANES] = y.T.astype(o_ref.dtype)


def _diff_attn(qk, vt, lq1, lk1, lq2, lk2, g, batch, seq, lam_init):
    t = qk.shape[0]
    tq = TKV
    nq = seq // tq
    bias = _diff_bias_tiles()
    vec = lambda a: pl.BlockSpec(a.shape, lambda b, i: (0, 0))
    args = [a.reshape(1, -1) for a in (lq1, lk1, lq2, lk2)] + [g.reshape(-1, 1)]
    return pl.pallas_call(
        partial(_diff_attn_kernel, tq=tq, lam_init=lam_init),
        grid=(batch, nq),
        in_specs=[pl.BlockSpec((tq, DIFF_WIDTH), lambda b, i: (b * nq + i, 0)),
                  pl.BlockSpec((seq, DIFF_WIDTH), lambda b, i: (b, 1)),
                  pl.BlockSpec((nq,) + vt.shape[1:], lambda b, i: (b, 0, 0)),
                  pl.BlockSpec(bias.shape, lambda b, i: (0, 0, 0, 0))] + [vec(a) for a in args],
        out_specs=pl.BlockSpec((tq, DIFF_WIDTH), lambda b, i: (b * nq + i, 0)),
        out_shape=jax.ShapeDtypeStruct((t, DIFF_WIDTH), BF16),
        scratch_shapes=[pltpu.VMEM((2 * DIFF_HEADS, tq, LANES), BF16),
                        pltpu.VMEM((2 * DIFF_HEADS, tq), F32),
                        pltpu.VMEM((2 * DIFF_HEADS, DIFF_VT_ROWS, tq), F32)],
        compiler_params=_cp(2),
        name="diff_attn",
    )(qk, qk, vt, bias, *args)


def _chunk_attn_kernel(q_ref, k_ref, vt_ref, bias_ref, o_ref, qm_ref, stage_ref, *, tq):
    qi = pl.program_id(1)
    lo, hi = _half_masks(tq)
    scale = CHK_HEAD_DIM ** -0.5 * LOG2E
    for pair in range(CHK_HEADS // 2):
        q0, q1 = _masked_halves(q_ref[:, pair * LANES:(pair + 1) * LANES], scale, lo, hi)
        qm_ref[2 * pair] = q0
        qm_ref[2 * pair + 1] = q1

    def window(n_tiles):
        slots = range(CHK_WIN_TILES - n_tiles, CHK_WIN_TILES)

        def scores(hd):
            cols = slice((hd // 2) * LANES, (hd // 2 + 1) * LANES)
            out = []
            for w in slots:
                ks = pl.multiple_of((qi - (CHK_WIN_TILES - 1) + w) * tq, tq)
                out.append(_nt_dot(k_ref[pl.ds(ks, tq), cols], qm_ref[hd]) + bias_ref[hd, w])
            return out

        def softmax(hd, s_tiles):
            m = jnp.max(s_tiles[0], axis=0, keepdims=True)
            for s in s_tiles[1:]:
                m = jnp.maximum(m, jnp.max(s, axis=0, keepdims=True))
            return None, [jnp.exp2(s - m).astype(BF16) for s in s_tiles]

        def accumulate(hd, _, p_tiles):
            rows = slice(hd * CHK_VT_ROWS, (hd + 1) * CHK_VT_ROWS)
            acc = None
            for w, p in zip(slots, p_tiles):
                pv = jnp.dot(vt_ref[qi - (CHK_WIN_TILES - 1) + w, rows, :], p, preferred_element_type=F32)
                acc = pv if acc is None else acc + pv
            stage_ref[hd] = acc[0:CHK_HEAD_DIM, :] * (1.0 / acc[CHK_HEAD_DIM:CHK_HEAD_DIM + 1, :])

        _run_pipelined(list(range(CHK_HEADS)), scores, softmax, accumulate)

    for n_tiles in range(1, CHK_WIN_TILES):
        pl.when(qi == n_tiles - 1)(partial(window, n_tiles))
    pl.when(qi >= CHK_WIN_TILES - 1)(partial(window, CHK_WIN_TILES))

    for pair in range(CHK_HEADS // 2):
        both = jnp.concatenate([stage_ref[2 * pair], stage_ref[2 * pair + 1]], axis=0)
        o_ref[:, pair * LANES:(pair + 1) * LANES] = both.T.astype(o_ref.dtype)


def _chunk_attn(qk, vt, bias, layer, batch, seq):
    t = qk.shape[0]
    tq = TKV
    nq = seq // tq
    return pl.pallas_call(
        partial(_chunk_attn_kernel, tq=tq),
        grid=(batch, nq),
        in_specs=[pl.BlockSpec((tq, CHK_WIDTH), lambda b, i: (b * nq + i, 0)),
                  pl.BlockSpec((seq, CHK_WIDTH), lambda b, i: (b, 1)),
                  pl.BlockSpec((nq,) + vt.shape[1:], lambda b, i: (b, 0, 0)),
                  pl.BlockSpec((None,) + bias.shape[1:], lambda b, i: (layer, 0, 0, 0, 0))],
        out_specs=pl.BlockSpec((tq, CHK_WIDTH), lambda b, i: (b * nq + i, 0)),
        out_shape=jax.ShapeDtypeStruct((t, CHK_WIDTH), BF16),
        scratch_shapes=[pltpu.VMEM((CHK_HEADS, tq, LANES), BF16),
                        pltpu.VMEM((CHK_HEADS, CHK_HEAD_DIM, tq), F32)],
        compiler_params=_cp(2),
        name="chunk_attn",
    )(qk, qk, vt, bias)


def _chunk_bias(rel_tables):
    n_layers, n_heads, _ = rel_tables.shape
    period = 2 * TKV
    ext = jnp.concatenate([rel_tables[..., REL_CLIP - TKV:],
                           jnp.broadcast_to(rel_tables[..., 2 * REL_CLIP:], (n_layers, n_heads, 3 * TKV - 1 - REL_CLIP))],
                          axis=-1) * LOG2E
    seqs, bands = [], []
    kj = np.arange(TKV)[:, None]
    qi = np.arange(TKV)[None, :]
    for w in range(CHK_WIN_TILES):
        dist = (CHK_WIN_TILES - 1 - w) * TKV
        seqs.append(jnp.concatenate([ext[..., dist + TKV:dist + 2 * TKV], ext[..., dist:dist + TKV]], axis=-1))
        dchunk = (qi + dist) // CHUNK - kj // CHUNK
        bands.append((dchunk >= 0) & (dchunk <= CHK_LEFT_CHUNKS))
    seq = jnp.stack(seqs, axis=2).reshape(-1, period)
    flat = jnp.tile(seq, (1, TKV))[:, :TKV * (period - 1)]
    bias = flat.reshape(n_layers, n_heads, CHK_WIN_TILES, TKV, period - 1)[..., :TKV]
    return jnp.where(jnp.asarray(np.stack(bands)), bias, NEG).astype(F32)


def _mla_attn_kernel(q_ref, k_ref, vt_ref, o_ref, m_ref, acc_ref, *, tq):
    qi = pl.program_id(1)
    m_ref[...] = jnp.full(m_ref.shape, NEG, F32)
    acc_ref[...] = jnp.zeros(acc_ref.shape, F32)
    ki = lax.broadcasted_iota(jnp.int32, (tq, tq), 0)
    qj = lax.broadcasted_iota(jnp.int32, (tq, tq), 1)
    diag_ok = (ki // CHUNK) <= (qj // CHUNK)

    def scores(unit):
        kt, hd, masked = unit
        cols = slice(hd * LANES, (hd + 1) * LANES)
        s = _nt_dot(k_ref[pl.ds(pl.multiple_of(kt * tq, tq), tq), cols], q_ref[:, cols])
        return jnp.where(diag_ok, s, NEG) if masked else s

    def softmax_pv(unit, s):
        kt, hd, _ = unit
        m_prev = m_ref[hd:hd + 1, :]
        m_new = jnp.maximum(m_prev, jnp.max(s, axis=0, keepdims=True))
        m_ref[hd:hd + 1, :] = m_new
        p = jnp.exp2(s - m_new).astype(BF16)
        return jnp.exp2(m_prev - m_new), p

    def accumulate(unit, alpha, p):
        kt, hd, _ = unit
        pv = jnp.dot(vt_ref[kt, hd * MLA_VT_ROWS:(hd + 1) * MLA_VT_ROWS, :], p,
                     preferred_element_type=F32)
        acc_ref[hd] = alpha * acc_ref[hd] + pv

    def tiles(kts):
        _run_pipelined([(kt, hd, masked) for kt, masked in kts for hd in range(MLA_HEADS)],
                       scores, softmax_pv, accumulate)

    _causal_tile_loop(qi, tiles)

    for pair in range(MLA_HEADS // 2):
        halves = []
        for hd in (2 * pair, 2 * pair + 1):
            acc = acc_ref[hd]
            halves.append(acc[0:MLA_V, :] * (1.0 / acc[MLA_V:MLA_V + 1, :]))
        o_ref[:, pair * LANES:(pair + 1) * LANES] = jnp.concatenate(halves, axis=0).T.astype(o_ref.dtype)


def _mla_attn(q, k, vt, batch, seq):
    t = q.shape[0]
    tq = TKV
    nq = seq // tq
    return pl.pallas_call(
        partial(_mla_attn_kernel, tq=tq),
        grid=(batch, nq),
        in_specs=[pl.BlockSpec((tq, q.shape[1]), lambda b, i: (b * nq + i, 0)),
                  pl.BlockSpec((seq, k.shape[1]), lambda b, i: (b, 0)),
                  pl.BlockSpec((nq,) + vt.shape[1:], lambda b, i: (b, 0, 0))],
        out_specs=pl.BlockSpec((tq, MLA_HEADS * MLA_V), lambda b, i: (b * nq + i, 0)),
        out_shape=jax.ShapeDtypeStruct((t, MLA_HEADS * MLA_V), BF16),
        scratch_shapes=[pltpu.VMEM((MLA_HEADS, tq), F32),
                        pltpu.VMEM((MLA_HEADS, MLA_VT_ROWS, tq), F32)],
        compiler_params=_cp(2),
        name="mla_attn",
    )(q, k, vt)


def _route(logits_t, bias_ref):
    e_rows = [logits_t[e:e + 1, :] for e in range(N_EXPERTS)]
    scores = [jax.nn.sigmoid(r) for r in e_rows]
    sel = [scores[e] + bias_ref[e] for e in range(N_EXPERTS)]
    gscore = []
    for g in range(N_GROUPS):
        s4 = sel[g * EXPERTS_PER_GROUP:(g + 1) * EXPERTS_PER_GROUP]
        best = None
        for a in range(EXPERTS_PER_GROUP):
            for b in range(a + 1, EXPERTS_PER_GROUP):
                pair = s4[a] + s4[b]
                best = pair if best is None else jnp.maximum(best, pair)
        gscore.append(best)
    gbest = gscore[0]
    gidx = jnp.zeros_like(gbest, dtype=jnp.int32)
    for g in range(1, N_GROUPS):
        better = gscore[g] > gbest
        gbest = jnp.where(better, gscore[g], gbest)
        gidx = jnp.where(better, g, gidx)
    masked = [jnp.where(gidx == e // EXPERTS_PER_GROUP, sel[e], -jnp.inf) for e in range(N_EXPERTS)]

    def arg_first_max(vals):
        top = vals[0]
        for v in vals[1:]:
            top = jnp.maximum(top, v)
        idx = jnp.full(top.shape, N_EXPERTS, jnp.int32)
        for e in range(N_EXPERTS - 1, -1, -1):
            idx = jnp.where(vals[e] == top, e, idx)
        return idx

    i0 = arg_first_max(masked)
    i1 = arg_first_max([jnp.where(i0 == e, -jnp.inf, masked[e]) for e in range(N_EXPERTS)])
    w0 = sum(jnp.where(i0 == e, scores[e], 0.0) for e in range(N_EXPERTS))
    w1 = sum(jnp.where(i1 == e, scores[e], 0.0) for e in range(N_EXPERTS))
    den = w0 + w1
    w0 = w0 / den
    w1 = w1 / den
    rows = [jnp.where(i0 == e, w0, 0.0) + jnp.where(i1 == e, w1, 0.0) for e in range(N_EXPERTS)]
    return jnp.concatenate(rows, axis=0)


def _merge_kernel(ya_ref, yb_ref, yc_ref, hm_ref, x_ref, gt_ref, sc_ref, sh_ref, g_ref,
                  wa_ref, wb_ref, wc_ref, wo_ref, wgate_ref, wrh_ref, wrl_ref, rb_ref,
                  xo_ref, h_ref, comb_ref):
    d = D_MODEL
    tm = x_ref.shape[0]
    halves = [slice(0, tm // 2), slice(tm // 2, tm)]
    dot = partial(jnp.dot, preferred_element_type=F32)
    merged = []
    for r in halves:
        branch = (dot(ya_ref[r, :], wa_ref[...]), dot(yb_ref[r, :], wb_ref[...]), dot(yc_ref[r, :], wc_ref[...]))
        total = None
        for k, y in enumerate(branch):
            term = jax.nn.sigmoid(dot(hm_ref[r, :], wgate_ref[:, k * d:(k + 1) * d])) * y
            total = term if total is None else total + term
        merged.append(total.astype(BF16))
    ys = [dot(m, wo_ref[...]) for m in merged]
    h2s = []
    for r, y in zip(halves, ys):
        x_new = x_ref[r, :] + gt_ref[0] * y
        xo_ref[r, :] = x_new
        h2 = _rms(x_new, g_ref[...]) * (1.0 + sc_ref[0]) + sh_ref[0]
        h_ref[r, :] = h2.astype(h_ref.dtype)
        h2s.append(h2)
    logits = []
    for h2 in h2s:
        hi = h2.astype(BF16)
        lo = (h2 - hi.astype(F32)).astype(BF16)
        logits.append(dot(hi, wrh_ref[...]) + (dot(lo, wrh_ref[...]) + dot(hi, wrl_ref[...])))
    pad_rows = jnp.zeros((LANES - N_EXPERTS, tm // 2), F32)
    for r, lg in zip(halves, logits):
        comb_t = _route(lg.T[0:N_EXPERTS, :], rb_ref)
        comb_ref[r, :] = jnp.concatenate([comb_t, pad_rows], axis=0).T


def _merge(ya, yb, yc, h, x2, gt, sc, sh, g, w, wr_hi, wr_lo, rbias, seq):
    t, d = x2.shape
    tm = TM
    tpb = seq // tm
    row = lambda n: pl.BlockSpec((tm, n), lambda i: (i, 0))
    full = lambda a: pl.BlockSpec(a.shape, lambda i: (0,) * a.ndim)
    bspec = pl.BlockSpec((1, 1, d), lambda i: (i // tpb, 0, 0))
    ws = [w["br_diff"], w["br_chunk"], w["br_mla"], w["out"], w["gate"], wr_hi, wr_lo]
    g2 = g.reshape(1, d)
    return pl.pallas_call(
        _merge_kernel,
        grid=(t // tm,),
        in_specs=[row(ya.shape[1]), row(yb.shape[1]), row(yc.shape[1]), row(h.shape[1]), row(d),
                  bspec, bspec, bspec, full(g2)] + [full(a) for a in ws]
                 + [pl.BlockSpec(memory_space=pltpu.SMEM)],
        out_specs=[row(d), row(d), row(LANES)],
        out_shape=[jax.ShapeDtypeStruct((t, d), F32), jax.ShapeDtypeStruct((t, d), BF16),
                   jax.ShapeDtypeStruct((t, LANES), F32)],
        compiler_params=_cp(1),
        name="merge_route",
    )(ya, yb, yc, h, x2, gt, sc, sh, g2, *ws, rbias)


def _moe_kernel(h_ref, wg_ref, wu_ref, wd_ref, comb_ref, x_ref, gt_ref, gf_ref, o_ref, acc_ref, *, final_norm):
    step = pl.program_id(1)
    n_sub, dff, d = wd_ref.shape

    @pl.when(step == 0)
    def _():
        acc_ref[...] = jnp.zeros_like(acc_ref)

    tm = h_ref.shape[0]
    halves = [slice(0, tm // 2), slice(tm // 2, tm)]
    dot = partial(jnp.dot, preferred_element_type=F32)
    lane = lax.broadcasted_iota(jnp.int32, comb_ref.shape, 1)
    comb = [jnp.sum(jnp.where(lane == step * n_sub + j, comb_ref[...], 0.0), axis=1, keepdims=True)
            for j in range(n_sub)]
    hidden = []
    for r in halves:
        parts = []
        for j in range(n_sub):
            a = dot(h_ref[r, :], wg_ref[j])
            b = dot(h_ref[r, :], wu_ref[j])
            parts.append((a * jax.nn.sigmoid(a) * b * comb[j][r, :]).astype(BF16))
        hidden.append(jnp.concatenate(parts, axis=1))
    wd = wd_ref[...].reshape(n_sub * dff, d)
    down = [dot(x, wd) for x in hidden]
    for r, y in zip(halves, down):
        acc_ref[r, :] += y

    @pl.when(step == pl.num_programs(1) - 1)
    def _():
        x_new = x_ref[...] + gt_ref[0] * acc_ref[...]
        o_ref[...] = _rms(x_new, gf_ref[...]) if final_norm else x_new


def _moe_dense(h2, comb, x2, gt, wg, wu, wd, layer, g_final, final_norm, seq):
    t, d = x2.shape
    tm = TM_MOE
    tpb = seq // tm
    _, n_e, _, dff = wg.shape
    n_sub = MOE_EXPERTS_PER_STEP
    return pl.pallas_call(
        partial(_moe_kernel, final_norm=final_norm),
        grid=(t // tm, n_e // n_sub),
        in_specs=[pl.BlockSpec((tm, d), lambda i, e: (i, 0)),
                  pl.BlockSpec((None, n_sub, d, dff), lambda i, e: (layer, e, 0, 0)),
                  pl.BlockSpec((None, n_sub, d, dff), lambda i, e: (layer, e, 0, 0)),
                  pl.BlockSpec((None, n_sub, dff, d), lambda i, e: (layer, e, 0, 0)),
                  pl.BlockSpec((tm, LANES), lambda i, e: (i, 0)),
                  pl.BlockSpec((tm, d), lambda i, e: (i, 0)),
                  pl.BlockSpec((1, 1, d), lambda i, e: (i // tpb, 0, 0)),
                  pl.BlockSpec((1, d), lambda i, e: (0, 0))],
        out_specs=pl.BlockSpec((tm, d), lambda i, e: (i, 0)),
        out_shape=jax.ShapeDtypeStruct((t, d), F32),
        scratch_shapes=[pltpu.VMEM((tm, d), F32)],
        compiler_params=_cp(2),
        name="moe_dense",
    )(h2, wg, wu, wd, comb, x2, gt, g_final.reshape(1, d))


def _pad_heads(w, n_heads, width, start, stop, at=0):
    rows = w.shape[0]
    wh = w.reshape(rows, n_heads, width)[:, :, start:stop]
    out = jnp.zeros((rows, n_heads, LANES), w.dtype)
    out = out.at[:, :, at:at + (stop - start)].set(wh)
    return out.reshape(rows, n_heads * LANES)


def _rope_swap(w_rope):
    half = MLA_ROPE // 2
    return jnp.concatenate([-w_rope[..., half:], w_rope[..., :half]], axis=-1)


def _layer_weights(l, w_in, mla_q_norm_g, mla_w_q_b, mla_kv_norm_g, mla_w_kv_b,
                   w_branch_diff, w_branch_chunk, w_branch_mla, w_out):
    wi = w_in[l]
    o_chunk = 3 * DIFF_WIDTH
    o_mq = o_chunk + 3 * CHK_WIDTH
    o_mkv = o_mq + MLA_Q_LORA
    o_gate = o_mkv + MLA_KV_LORA + MLA_ROPE
    qk = MLA_NOPE + MLA_ROPE
    wqb = mla_w_q_b[l]
    wq_rope = wqb.reshape(MLA_Q_LORA, MLA_HEADS, qk)[:, :, MLA_NOPE:]
    wqs = jnp.zeros((MLA_Q_LORA, MLA_HEADS, LANES), F32).at[:, :, MLA_NOPE:qk].set(_rope_swap(wq_rope))
    wkr = wi[:, o_mkv + MLA_KV_LORA:o_gate]
    pad_kr = lambda a: jnp.zeros((D_MODEL, LANES), F32).at[:, MLA_NOPE:qk].set(a)
    wkvb = mla_w_kv_b[l]
    bf = lambda a: a.astype(BF16)

    def vt_weight(wv, n_heads, v_dim):
        padded = jnp.pad(wv.reshape(D_MODEL, n_heads, v_dim), ((0, 0), (0, 0), (0, ONES_ROWS)))
        return bf(padded.reshape(D_MODEL, n_heads * (v_dim + ONES_ROWS)).T)

    return {
        "diff_qk": bf(wi[:, :2 * DIFF_WIDTH]),
        "diff_vt": vt_weight(wi[:, 2 * DIFF_WIDTH:o_chunk], DIFF_HEADS, DIFF_V_DIM),
        "chunk_qk": bf(wi[:, o_chunk:o_chunk + 2 * CHK_WIDTH]),
        "chunk_vt": vt_weight(wi[:, o_chunk + 2 * CHK_WIDTH:o_mq], CHK_HEADS, CHK_HEAD_DIM),
        "gate": bf(wi[:, o_gate:]),
        "mq": bf(wi[:, o_mq:o_mkv]),
        "ckv": bf(wi[:, o_mkv:o_mkv + MLA_KV_LORA]),
        "kr": bf(pad_kr(wkr)),
        "krs": bf(pad_kr(_rope_swap(wkr))),
        "gq": mla_q_norm_g[l].reshape(1, -1),
        "gkv": mla_kv_norm_g[l].reshape(1, -1),
        "q": bf(_pad_heads(wqb, MLA_HEADS, qk, 0, qk)),
        "qs": bf(wqs.reshape(MLA_Q_LORA, MLA_HEADS * LANES)),
        "k": bf(_pad_heads(wkvb, MLA_HEADS, MLA_NOPE + MLA_V, 0, MLA_NOPE)),
        "vt": bf(jnp.pad(wkvb.reshape(MLA_KV_LORA, MLA_HEADS, MLA_NOPE + MLA_V)[:, :, MLA_NOPE:],
                         ((0, 0), (0, 0), (0, ONES_ROWS))).reshape(MLA_KV_LORA, MLA_HEADS * MLA_VT_ROWS).T),
        "br_diff": bf(w_branch_diff[l]),
        "br_chunk": bf(w_branch_chunk[l]),
        "br_mla": bf(w_branch_mla[l]),
        "out": bf(w_out[l]),
    }


def _rope_tables(seq):
    pos = jnp.arange(seq, dtype=F32)
    inv_freq = 1.0 / (ROPE_THETA ** (jnp.arange(0, MLA_ROPE, 2, dtype=F32) / MLA_ROPE))
    ang = pos[:, None] * inv_freq[None, :]
    cos, sin = jnp.cos(ang), jnp.sin(ang)
    qk = MLA_NOPE + MLA_ROPE
    cos_tab = jnp.zeros((seq, LANES), F32).at[:, :MLA_NOPE].set(1.0)
    cos_tab = cos_tab.at[:, MLA_NOPE:qk].set(jnp.concatenate([cos, cos], axis=-1))
    sin_tab = jnp.zeros((seq, LANES), F32).at[:, MLA_NOPE:qk].set(jnp.concatenate([sin, sin], axis=-1))
    return cos_tab, sin_tab


def kernel(x, c, w_mod, b_mod, g_norm_mix, g_norm_ffn, w_in, diff_lambda_q1, diff_lambda_k1, diff_lambda_q2, diff_lambda_k2, diff_subln_g, chunk_rel_bias, mla_q_norm_g, mla_w_q_b, mla_kv_norm_g, mla_w_kv_b, w_branch_diff, w_branch_chunk, w_branch_mla, w_out, w_router, router_bias, w_exp_gate, w_exp_up, w_exp_down, g_final):
    batch, seq, d = x.shape
    depth = w_in.shape[0]
    t = batch * seq
    x2 = x.reshape(t, d)
    cos_tab, sin_tab = _rope_tables(seq)
    mod = _modulation(c, w_mod, b_mod)
    chunk_bias = _chunk_bias(chunk_rel_bias)
    wg_bf, wu_bf, wd_bf = (a.astype(BF16) for a in (w_exp_gate, w_exp_up, w_exp_down))
    wr_pad = jnp.pad(w_router, ((0, 0), (0, LANES - N_EXPERTS)))
    wr_hi = wr_pad.astype(BF16)
    wr_lo = (wr_pad - wr_hi.astype(F32)).astype(BF16)
    for l in range(depth):
        sh1, sc1, gt1, sh2, sc2, gt2 = [m.reshape(batch, 1, d) for m in jnp.split(mod[l], 6, axis=-1)]
        w = _layer_weights(l, w_in, mla_q_norm_g, mla_w_q_b, mla_kv_norm_g, mla_w_kv_b,
                           w_branch_diff, w_branch_chunk, w_branch_mla, w_out)
        h = _norm_mod(x2, g_norm_mix[l], sc1, sh1, seq)
        lam_init = 0.8 - 0.6 * math.exp(-0.3 * l)
        dqk, dvt = _qk_vt(h, w["diff_qk"], w["diff_vt"], DIFF_V_DIM, "diff_prep")
        ya = _diff_attn(dqk, dvt, diff_lambda_q1[l], diff_lambda_k1[l], diff_lambda_q2[l], diff_lambda_k2[l],
                        diff_subln_g[l], batch, seq, lam_init)
        cqk, cvt = _qk_vt(h, w["chunk_qk"], w["chunk_vt"], CHK_HEAD_DIM, "chunk_prep")
        yb = _chunk_attn(cqk, cvt, chunk_bias, l, batch, seq)
        q, k, v = _mla_prep(h, w, cos_tab, sin_tab, seq)
        yc = _mla_attn(q, k, v, batch, seq)
        x2, h2, comb = _merge(ya, yb, yc, h, x2, gt1, sc2, sh2, g_norm_ffn[l], w, wr_hi, wr_lo,
                                router_bias, seq)
        x2 = _moe_dense(h2, comb, x2, gt2, wg_bf, wu_bf, wd_bf, l, g_final, l == depth - 1, seq)
    return x2.reshape(batch, seq, d)
```
